```python
import jax, jax.numpy as jnp
from jax import lax
import numpy as np

D_MODEL = 4096
BATCH = 8
SEQ = 2048
DEPTH = 2
DEC_BATCH = 8
DEC_SEQ = 16
PAST_LEN = 2048

CHUNK = 64
N_EVEN = (DEPTH + 1) // 2
N_ODD = DEPTH // 2
D_A = D_MODEL // 2
N_A_GROUPS = 4
A_GROUP = D_A // N_A_GROUPS
GMLP_CHUNK = 128
D_B = D_MODEL // 2
POOL_WINDOWS = (2, 4, 8, 16)
N_B_GROUPS = len(POOL_WINDOWS)
B_GROUP = D_B // N_B_GROUPS
POOL_STATE = max(POOL_WINDOWS) - 1
D_IN_EVEN = 2 * D_A + D_B
N_HEADS = 32
HEAD_DIM = D_MODEL // N_HEADS
Q_BLOCK = 128
N_EXPERTS = 32
TOP_K = 4
D_EXPERT = D_MODEL // 2
SWIGLU_ALPHA = 1.702
SWIGLU_LIMIT = 7.0
MOE_BLOCK = 128
DN_ALPHA = (2 * DEPTH) ** 0.25
DN_BETA = (8 * DEPTH) ** -0.25
LN_EPS = 1e-5

kernel_name = "streaming_hybrid_gmlp_pool_stickbreak_moe_step"


def layer_norm(x, g, b):
    xf = x.astype(jnp.float32)
    mu = jnp.mean(xf, axis=-1, keepdims=True)
    xc = xf - mu
    var = jnp.mean(xc * xc, axis=-1, keepdims=True)
    return (xc * lax.rsqrt(var + LN_EPS) * g + b).astype(x.dtype)


def ada_modulation(c, w, b):
    m = (jax.nn.silu(c) @ w + b)[:, None, :]
    shift, scale, gate = jnp.split(m, 3, axis=-1)
    return shift, scale, gate


def post_norm_residual(x, y, gate, g, b):
    return layer_norm(DN_ALPHA * x + (1.0 + gate) * y, g, b)


def chunk_mlp(u, v, ws, bs):
    Bn, L, _ = v.shape
    n_chunks = -(-L // GMLP_CHUNK)
    pad = n_chunks * GMLP_CHUNK - L
    vp = jnp.pad(v, ((0, 0), (0, pad), (0, 0))).reshape(Bn, n_chunks, GMLP_CHUNK, N_A_GROUPS, A_GROUP)
    tri = jnp.tril(jnp.ones((GMLP_CHUNK, GMLP_CHUNK), dtype=bool))
    wm = jnp.where(tri, ws, 0.0).astype(v.dtype)
    mixed = jnp.einsum('gts,bnsgd->bntgd', wm, vp) + bs.T[None, None, :, :, None]
    mixed = mixed.reshape(Bn, n_chunks * GMLP_CHUNK, D_A)[:, :L]
    return u * mixed


def pool_mixer(p, hist, pos0, w_grp, scale):
    Bn, L, _ = p.shape
    full = jnp.concatenate([hist, p], axis=1)
    cs = jnp.cumsum(full.astype(jnp.float32), axis=1)
    cs = jnp.pad(cs, ((0, 0), (1, 0), (0, 0)))
    pos = pos0 + jnp.arange(L)
    means = []
    for gi, w in enumerate(POOL_WINDOWS):
        sl = slice(gi * B_GROUP, (gi + 1) * B_GROUP)
        win_sum = cs[:, POOL_STATE + 1:POOL_STATE + 1 + L, sl] - cs[:, POOL_STATE + 1 - w:POOL_STATE + 1 - w + L, sl]
        cnt = jnp.minimum(w, pos + 1).astype(jnp.float32)[None, :, None]
        means.append(win_sum / cnt)
    pooled = jnp.concatenate(means, axis=-1).astype(p.dtype) - p
    mixed = jnp.einsum('blgd,gde->blge', pooled.reshape(Bn, L, N_B_GROUPS, B_GROUP), w_grp).reshape(Bn, L, D_B)
    return mixed * scale, full[:, -POOL_STATE:]


def even_mixer(h, pool_hist, pos0, w_in, gln_g, gln_b, gws, gb, pool_w, pool_scale, w_out):
    Bn, L, _ = h.shape
    proj = h @ w_in
    uv = jax.nn.gelu(proj[..., :2 * D_A])
    u = uv[..., :D_A]
    v = layer_norm(uv[..., D_A:].reshape(Bn, L, N_A_GROUPS, A_GROUP), gln_g, gln_b).reshape(Bn, L, D_A)
    a_out = chunk_mlp(u, v, gws, gb)
    b_out, pool_new = pool_mixer(proj[..., 2 * D_A:], pool_hist, pos0, pool_w, pool_scale)
    y = jnp.concatenate([a_out, b_out], axis=-1) @ w_out
    return y, pool_new, v


def stick_breaking(q, k, v, q_start):
    nq, nk = q.shape[1], k.shape[1]
    z = jnp.einsum('bqhd,bkhd->bhqk', q, k).astype(jnp.float32) * (HEAD_DIM ** -0.5)
    mask = jnp.arange(nk)[None, :] < (q_start + jnp.arange(nq))[:, None]
    log_keep = jnp.where(mask, jax.nn.log_sigmoid(-z), 0.0)
    after = lax.cumsum(log_keep, axis=3, reverse=True)
    after = jnp.concatenate([after[..., 1:], jnp.zeros_like(after[..., :1])], axis=-1)
    w = jnp.where(mask, jnp.exp(jax.nn.log_sigmoid(z) + after), 0.0)
    return jnp.einsum('bhqk,bkhd->bqhd', w.astype(v.dtype), v)


def odd_mixer(h, cache_k, cache_v, w_qkv, w_o):
    Bn, L, _ = h.shape
    qkv = (h @ w_qkv).reshape(Bn, L, 3, N_HEADS, HEAD_DIM)
    q, k, v = qkv[:, :, 0], qkv[:, :, 1], qkv[:, :, 2]
    if cache_k is None:
        blocks = []
        for i in range(-(-L // Q_BLOCK)):
            s, e = i * Q_BLOCK, min((i + 1) * Q_BLOCK, L)
            blocks.append(stick_breaking(q[:, s:e], k[:, :e], v[:, :e], s))
        o = jnp.concatenate(blocks, axis=1)
    else:
        past = cache_k.shape[1]
        o = stick_breaking(q, jnp.concatenate([cache_k, k], axis=1), jnp.concatenate([cache_v, v], axis=1), past)
    return o.reshape(Bn, L, D_MODEL) @ w_o, k, v


def moe(x, w_r, b_r, w1, b1, w2, b2):
    Bn, L, D = x.shape
    T = Bn * L
    xt = x.reshape(T, D)
    logits = (xt @ w_r).astype(jnp.float32) + b_r
    top_val, top_idx = lax.top_k(logits, TOP_K)
    gates = jax.nn.softmax(top_val, axis=-1)
    n_assign = T * TOP_K
    flat_e = top_idx.reshape(-1)
    order = jnp.argsort(flat_e)
    e_sorted = flat_e[order]
    tok_sorted = (order // TOP_K).astype(jnp.int32)
    gate_sorted = gates.reshape(-1)[order]
    counts = jnp.bincount(flat_e, length=N_EXPERTS)
    padded = (counts + MOE_BLOCK - 1) // MOE_BLOCK * MOE_BLOCK
    start = jnp.cumsum(counts) - counts
    pad_end = jnp.cumsum(padded)
    pad_start = pad_end - padded
    dest = pad_start[e_sorted] + jnp.arange(n_assign) - start[e_sorted]
    n_blocks = -(-(n_assign + N_EXPERTS * (MOE_BLOCK - 1)) // MOE_BLOCK)
    n_rows = n_blocks * MOE_BLOCK
    row_tok = jnp.full((n_rows,), T, jnp.int32).at[dest].set(tok_sorted)
    row_gate = jnp.zeros((n_rows,), jnp.float32).at[dest].set(gate_sorted)
    block_e = jnp.minimum(jnp.searchsorted(pad_end, jnp.arange(n_blocks) * MOE_BLOCK, side='right'), N_EXPERTS - 1)
    x_pad = jnp.concatenate([xt, jnp.zeros((1, D), xt.dtype)], axis=0)

    def expert_block(args):
        toks, e = args
        hg = x_pad[toks] @ w1[e] + b1[e]
        glu = jnp.minimum(hg[:, 0::2], SWIGLU_LIMIT)
        lin = jnp.clip(hg[:, 1::2], -SWIGLU_LIMIT, SWIGLU_LIMIT)
        act = glu * jax.nn.sigmoid(SWIGLU_ALPHA * glu) * (lin + 1.0)
        return act @ w2[e] + b2[e]

    y_rows = lax.map(expert_block, (row_tok.reshape(n_blocks, MOE_BLOCK), block_e)).reshape(n_rows, D)
    y_rows = y_rows * row_gate[:, None].astype(y_rows.dtype)
    y = jnp.zeros((T + 1, D), y_rows.dtype).at[row_tok].add(y_rows)[:T]
    return y.reshape(Bn, L, D).astype(x.dtype)


def setup_inputs(seed: int = 0) -> dict:
    key = jax.random.key(seed)
    keys = iter(jax.random.split(key, 40))

    def nrm(shape, scale):
        return jax.random.normal(next(keys), shape, jnp.float32) * scale

    d = D_MODEL
    inv_d = d ** -0.5
    w_qkv = jnp.concatenate([nrm((N_ODD, d, 2 * d), inv_d), nrm((N_ODD, d, d), inv_d * DN_BETA)], axis=-1)
    return {
        "x_prompt": nrm((BATCH, SEQ, d), 1.0),
        "x_sample": nrm((DEC_BATCH, DEC_SEQ, d), 1.0),
        "state_pool": nrm((N_EVEN, DEC_BATCH, POOL_STATE, D_B), 1.0),
        "cache_k": nrm((N_ODD, DEC_BATCH, PAST_LEN, N_HEADS, HEAD_DIM), 1.0),
        "cache_v": nrm((N_ODD, DEC_BATCH, PAST_LEN, N_HEADS, HEAD_DIM), DN_BETA),
        "c_prompt": nrm((BATCH, d), 1.0),
        "c_sample": nrm((DEC_BATCH, d), 1.0),
        "ada_w": nrm((DEPTH, 2, d, 3 * d), 0.1 * inv_d),
        "ada_b": nrm((DEPTH, 2, 3 * d), 0.02),
        "ln_g": 1.0 + nrm((DEPTH, 2, d), 0.02),
        "ln_b": nrm((DEPTH, 2, d), 0.02),
        "even_w_in": nrm((N_EVEN, d, D_IN_EVEN), inv_d),
        "gmlp_ln_g": 1.0 + nrm((N_EVEN, N_A_GROUPS, A_GROUP), 0.02),
        "gmlp_ln_b": nrm((N_EVEN, N_A_GROUPS, A_GROUP), 0.02),
        "gmlp_ws": nrm((N_EVEN, N_A_GROUPS, GMLP_CHUNK, GMLP_CHUNK), GMLP_CHUNK ** -0.5),
        "gmlp_b": 1.0 + nrm((N_EVEN, N_A_GROUPS, GMLP_CHUNK), 0.02),
        "pool_w": nrm((N_EVEN, N_B_GROUPS, B_GROUP, B_GROUP), B_GROUP ** -0.5),
        "pool_scale": 1.0 + nrm((N_EVEN, D_B), 0.02),
        "even_w_out": nrm((N_EVEN, D_A + D_B, d), (D_A + D_B) ** -0.5 * DN_BETA),
        "odd_w_qkv": w_qkv,
        "odd_w_o": nrm((N_ODD, d, d), inv_d * DN_BETA),
        "router_w": nrm((DEPTH, d, N_EXPERTS), inv_d),
        "router_b": nrm((DEPTH, N_EXPERTS), 0.01),
        "moe_w1": nrm((DEPTH, N_EXPERTS, d, 2 * D_EXPERT), inv_d),
        "moe_b1": nrm((DEPTH, N_EXPERTS, 2 * D_EXPERT), 0.01),
        "moe_w2": nrm((DEPTH, N_EXPERTS, D_EXPERT, d), D_EXPERT ** -0.5 * DN_BETA),
        "moe_b2": nrm((DEPTH, N_EXPERTS, d), 0.01),
    }


def reference(x_prompt, x_sample, state_pool, cache_k, cache_v, c_prompt, c_sample,
              ada_w, ada_b, ln_g, ln_b, even_w_in, gmlp_ln_g, gmlp_ln_b, gmlp_ws, gmlp_b,
              pool_w, pool_scale, even_w_out, odd_w_qkv, odd_w_o,
              router_w, router_b, moe_w1, moe_b1, moe_w2, moe_b2):

    def trunk(x, c, pos0, pool_hist, ck, cv):
        pools, gvs, ks, vs = [], [], [], []
        for layer in range(DEPTH):
            j = layer // 2
            shift, scale, gate = ada_modulation(c, ada_w[layer, 0], ada_b[layer, 0])
            h = x * (1.0 + scale) + shift
            if layer % 2 == 0:
                hist = jnp.zeros((x.shape[0], POOL_STATE, D_B), x.dtype) if pool_hist is None else pool_hist[j]
                y, p_new, g_v = even_mixer(h, hist, pos0, even_w_in[j], gmlp_ln_g[j], gmlp_ln_b[j],
                                           gmlp_ws[j], gmlp_b[j], pool_w[j], pool_scale[j], even_w_out[j])
                pools.append(p_new)
                gvs.append(g_v)
            else:
                y, k_new, v_new = odd_mixer(h, None if ck is None else ck[j], None if cv is None else cv[j],
                                            odd_w_qkv[j], odd_w_o[j])
                ks.append(k_new)
                vs.append(v_new)
            x = post_norm_residual(x, y, gate, ln_g[layer, 0], ln_b[layer, 0])
            shift, scale, gate = ada_modulation(c, ada_w[layer, 1], ada_b[layer, 1])
            y = moe(x * (1.0 + scale) + shift, router_w[layer], router_b[layer],
                    moe_w1[layer], moe_b1[layer], moe_w2[layer], moe_b2[layer])
            x = post_norm_residual(x, y, gate, ln_g[layer, 1], ln_b[layer, 1])
        return x, jnp.stack(pools), jnp.stack(gvs), jnp.stack(ks), jnp.stack(vs)

    y_prompt, pool_prompt, _, k_prompt, v_prompt = trunk(x_prompt, c_prompt, 0, None, None, None)
    y_sample, pool_sample, gmlp_v_sample, k_sample, v_sample = trunk(
        x_sample, c_sample, PAST_LEN, state_pool, cache_k, cache_v)
    return (y_prompt, y_sample, pool_prompt, pool_sample, gmlp_v_sample, k_prompt, v_prompt, k_sample, v_sample)
```

```python
import functools

import jax
import jax.numpy as jnp
from jax import lax
from jax.experimental import pallas as pl
from jax.experimental.pallas import tpu as pltpu

F32 = jnp.float32
BF16 = jnp.bfloat16

POOL_WINDOWS = (2, 4, 8, 16)
POOL_HALO = 16
TOP_K = 4
GMLP_CHUNK = 128
SWIGLU_ALPHA = 1.702
SWIGLU_LIMIT = 7.0
LN_EPS = 1e-5

V7X_VMEM_LIMIT_BYTES = 56 * 1024 * 1024
V7X_MXU_DIM = 256

MOE_ROWS = 512
ATTN_DEAD_LOG = -110.0


def _tile(n, pref, mult=8):
    if n <= pref:
        return n
    t = (pref // mult) * mult
    while t >= mult:
        if n % t == 0:
            return t
        t -= mult
    return n


def _params(*sem):
    return pltpu.CompilerParams(dimension_semantics=sem, vmem_limit_bytes=V7X_VMEM_LIMIT_BYTES)


def _layer_norm_rows(t, g, b):
    mu = jnp.mean(t, axis=-1, keepdims=True)
    tc = t - mu
    var = jnp.mean(tc * tc, axis=-1, keepdims=True)
    return tc * lax.rsqrt(var + LN_EPS) * g + b


def _ada_kernel(c_ref, w_ref, b_ref, o_ref):
    c = c_ref[...]
    s = (c * jax.nn.sigmoid(c)).astype(BF16)
    o_ref[...] = jnp.dot(s, w_ref[...].astype(BF16), preferred_element_type=F32) + b_ref[...]


def _ada(c_all, ada_w, ada_b):
    nb, d = c_all.shape
    n = ada_w.shape[-1]
    w = ada_w.reshape(-1, d, n)
    b = ada_b.reshape(-1, 1, n)
    npair = w.shape[0]
    tn = _tile(n, 512, 128)
    return pl.pallas_call(
        _ada_kernel,
        grid=(npair, n // tn),
        in_specs=[pl.BlockSpec((nb, d), lambda p, j: (0, 0)),
                  pl.BlockSpec((None, d, tn), lambda p, j: (p, 0, j)),
                  pl.BlockSpec((None, 1, tn), lambda p, j: (p, 0, j))],
        out_specs=pl.BlockSpec((None, nb, tn), lambda p, j: (p, 0, j)),
        out_shape=jax.ShapeDtypeStruct((npair, nb, n), F32),
        compiler_params=_params("arbitrary", "arbitrary"),
        name="ada_modulation",
    )(c_all, w, b)


def _mod_kernel(x_ref, m_ref, h_ref):
    h_ref[...] = (x_ref[...] * (1.0 + m_ref[1:2, :]) + m_ref[0:1, :]).astype(h_ref.dtype)


def _modulate(x, mod, out_dtype):
    bn, l, d = x.shape
    tl = _tile(l, 256)
    return pl.pallas_call(
        _mod_kernel,
        grid=(bn, l // tl),
        in_specs=[pl.BlockSpec((None, tl, d), lambda b, i: (b, i, 0)),
                  pl.BlockSpec((None, 3, d), lambda b, i: (b, 0, 0))],
        out_specs=pl.BlockSpec((None, tl, d), lambda b, i: (b, i, 0)),
        out_shape=jax.ShapeDtypeStruct((bn, l, d), out_dtype),
        compiler_params=_params("arbitrary", "arbitrary"),
        name="modulate",
    )(x, mod)


def _mm_kernel(a_ref, w_ref, o_ref, wb_ref, *, act):
    @pl.when(pl.program_id(1) == 0)
    def _():
        wb_ref[...] = w_ref[...].astype(BF16)

    acc = jnp.dot(a_ref[...], wb_ref[...], preferred_element_type=F32)
    if act == "gelu":
        acc = jax.nn.gelu(acc, approximate=True)
    o_ref[...] = acc.astype(o_ref.dtype)


def _mm(a, w, col0, n, *, act=None, out_dtype=F32, name="matmul"):
    m, k = a.shape
    tm = _tile(m, 1024)
    tn = _tile(n, 512, 128)
    assert col0 % tn == 0
    joff = col0 // tn
    return pl.pallas_call(
        functools.partial(_mm_kernel, act=act),
        grid=(n // tn, m // tm),
        in_specs=[pl.BlockSpec((tm, k), lambda j, i: (i, 0)),
                  pl.BlockSpec((k, tn), lambda j, i: (0, j + joff))],
        out_specs=pl.BlockSpec((tm, tn), lambda j, i: (i, j)),
        out_shape=jax.ShapeDtypeStruct((m, n), out_dtype),
        scratch_shapes=[pltpu.VMEM((k, tn), BF16)],
        compiler_params=_params("arbitrary", "arbitrary"),
        name=name,
    )(a, w)


def _even_kernel(u_ref, v_ref, p_ref, pprev_ref, hist_ref, glng_ref, glnb_ref, ws_ref, gbt_ref,
                 pw_ref, ps_ref, *rest, tl, chunk, n_groups, pos0, emit_v):
    if emit_v:
        ab_ref, gv_ref, pwb_ref = rest
    else:
        ab_ref, pwb_ref = rest
        gv_ref = None
    b = pl.program_id(0)
    i = pl.program_id(1)
    d_a = u_ref.shape[-1]
    d_b = p_ref.shape[-1]
    ag = d_a // n_groups
    bg = d_b // len(POOL_WINDOWS)

    @pl.when((b == 0) & (i == 0))
    def _():
        pwb_ref[...] = pw_ref[...].astype(BF16)

    tri = (lax.broadcasted_iota(jnp.int32, (chunk, chunk), 0)
           >= lax.broadcasted_iota(jnp.int32, (chunk, chunk), 1))
    for g in range(n_groups):
        sl = slice(g * ag, (g + 1) * ag)
        vn = _layer_norm_rows(v_ref[:, sl], glng_ref[:, sl], glnb_ref[:, sl])
        if emit_v:
            gv_ref[:, sl] = vn
        wm = jnp.where(tri, ws_ref[g][:chunk, :chunk], 0.0).astype(BF16)
        vb = vn.astype(BF16)
        bias = gbt_ref[:chunk, g:g + 1]
        for c in range(tl // chunk):
            rows = slice(c * chunk, (c + 1) * chunk)
            mixed = jnp.dot(wm, vb[rows], preferred_element_type=F32) + bias
            ab_ref[rows, sl] = (u_ref[rows, sl] * mixed).astype(ab_ref.dtype)

    prev = jnp.where(i == 0, hist_ref[...], pprev_ref[...])
    p = p_ref[...]
    ext = jnp.concatenate([prev, p], axis=0)
    pos = pos0 + i * tl + lax.broadcasted_iota(jnp.int32, (tl, 1), 0)
    for gi, w in enumerate(POOL_WINDOWS):
        sl = slice(gi * bg, (gi + 1) * bg)
        s = ext[:, sl]
        step = 1
        while step < w:
            s = s + pltpu.roll(s, step, 0)
            step *= 2
        cnt = jnp.minimum(w, pos + 1).astype(F32)
        pooled = s[POOL_HALO:, :] / cnt - p[:, sl]
        mixed = jnp.dot(pooled.astype(BF16), pwb_ref[gi], preferred_element_type=F32) * ps_ref[:, sl]
        ab_ref[:, d_a + gi * bg:d_a + (gi + 1) * bg] = mixed.astype(ab_ref.dtype)


def _even_mix(u, v, p, hist, gln_g, gln_b, gws, gb, pool_w, pool_scale, *, pos0, emit_v):
    bn, l, d_a = u.shape
    d_b = p.shape[-1]
    n_groups = gws.shape[0]
    chunk = min(l, GMLP_CHUNK)
    tl = _tile(l, 2 * GMLP_CHUNK, chunk)
    assert l % chunk == 0 and tl % chunk == 0 and tl % POOL_HALO == 0
    halo_blocks = tl // POOL_HALO
    hist16 = jnp.concatenate(
        [jnp.zeros((bn, POOL_HALO - hist.shape[1], d_b), F32), hist], axis=1)
    out_shape = [jax.ShapeDtypeStruct((bn, l, d_a + d_b), BF16)]
    out_specs = [pl.BlockSpec((None, tl, d_a + d_b), lambda b, i: (b, i, 0))]
    if emit_v:
        out_shape.append(jax.ShapeDtypeStruct((bn, l, d_a), F32))
        out_specs.append(pl.BlockSpec((None, tl, d_a), lambda b, i: (b, i, 0)))
    row_blk = lambda b, i: (b, i, 0)
    whole2 = lambda b, i: (0, 0)
    whole3 = lambda b, i: (0, 0, 0)
    res = pl.pallas_call(
        functools.partial(_even_kernel, tl=tl, chunk=chunk, n_groups=n_groups, pos0=pos0, emit_v=emit_v),
        grid=(bn, l // tl),
        in_specs=[pl.BlockSpec((None, tl, d_a), row_blk),
                  pl.BlockSpec((None, tl, d_a), row_blk),
                  pl.BlockSpec((None, tl, d_b), row_blk),
                  pl.BlockSpec((None, POOL_HALO, d_b),
                               lambda b, i: (b, jnp.maximum(i * halo_blocks - 1, 0), 0)),
                  pl.BlockSpec((None, POOL_HALO, d_b), lambda b, i: (b, 0, 0)),
                  pl.BlockSpec((1, d_a), whole2),
                  pl.BlockSpec((1, d_a), whole2),
                  pl.BlockSpec(gws.shape, whole3),
                  pl.BlockSpec((gb.shape[1], gb.shape[0]), whole2),
                  pl.BlockSpec(pool_w.shape, whole3),
                  pl.BlockSpec((1, d_b), whole2)],
        out_specs=out_specs,
        out_shape=out_shape,
        scratch_shapes=[pltpu.VMEM(pool_w.shape, BF16)],
        compiler_params=_params("arbitrary", "arbitrary"),
        name="even_mixer",
    )(u, v, p, p, hist16, gln_g.reshape(1, d_a), gln_b.reshape(1, d_a), gws, gb.T, pool_w,
      pool_scale.reshape(1, d_b))
    return res if emit_v else (res[0], None)


def _post_norm(x, y, gate, g, b, alpha):
    return _layer_norm_rows(alpha * x + (1.0 + gate) * y, g, b)


def _norm_mod_store(x, y, m, g, b, mn, xo_ref, h_ref, rows, alpha):
    xn = _post_norm(x, y, m[2:3, :], g, b, alpha)
    xo_ref[rows, :] = xn
    if h_ref is not None:
        h_ref[rows, :] = (xn * (1.0 + mn[1:2, :]) + mn[0:1, :]).astype(h_ref.dtype)


def _tile_plan(t_p, l_p, bs, ls):
    tl = bs * ls
    assert l_p % tl == 0 and tl % 8 == 0
    n_p = t_p // tl
    per_stream = l_p // tl
    p_blk = lambda s: (jnp.minimum(s, n_p - 1), 0)
    p_mod = lambda s: (jnp.minimum(s, n_p - 1) // per_stream, 0, 0)
    return tl, n_p, p_blk, p_mod


def _lnres_kernel(xp_ref, yp_ref, mp_ref, xs_ref, ys_ref, ms_ref, g_ref, b_ref, mpn_ref, msn_ref,
                  xpo_ref, xso_ref, h_ref, *, n_p, bs, ls, alpha):
    s = pl.program_id(0)

    @pl.when(s < n_p)
    def _():
        _norm_mod_store(xp_ref[...], yp_ref[...], mp_ref[...], g_ref[...], b_ref[...], mpn_ref[...],
                        xpo_ref, h_ref, slice(None), alpha)

    @pl.when(s == n_p)
    def _():
        for q in range(bs):
            rows = slice(q * ls, (q + 1) * ls)
            _norm_mod_store(xs_ref[rows, :], ys_ref[rows, :], ms_ref[q], g_ref[...], b_ref[...], msn_ref[q],
                            xso_ref, h_ref, rows, alpha)


def _lnres(xp, yp, mp, xs, ys, ms, g, b, mpn, msn, *, l_p, alpha, h_dtype):
    t_p, d = xp.shape
    bs = ms.shape[0]
    ls = xs.shape[0] // bs
    tl, n_p, p_blk, p_mod = _tile_plan(t_p, l_p, bs, ls)
    whole2 = lambda s: (0, 0)
    whole3 = lambda s: (0, 0, 0)
    return pl.pallas_call(
        functools.partial(_lnres_kernel, n_p=n_p, bs=bs, ls=ls, alpha=alpha),
        grid=(n_p + 1,),
        in_specs=[pl.BlockSpec((tl, d), p_blk), pl.BlockSpec((tl, d), p_blk),
                  pl.BlockSpec((None, 3, d), p_mod),
                  pl.BlockSpec((tl, d), whole2), pl.BlockSpec((tl, d), whole2),
                  pl.BlockSpec((bs, 3, d), whole3),
                  pl.BlockSpec((1, d), whole2), pl.BlockSpec((1, d), whole2),
                  pl.BlockSpec((None, 3, d), p_mod), pl.BlockSpec((bs, 3, d), whole3)],
        out_specs=[pl.BlockSpec((tl, d), p_blk), pl.BlockSpec((tl, d), whole2),
                   pl.BlockSpec((tl, d), lambda s: (s, 0))],
        out_shape=[jax.ShapeDtypeStruct((t_p, d), F32), jax.ShapeDtypeStruct((tl, d), F32),
                   jax.ShapeDtypeStruct((t_p + tl, d), h_dtype)],
        compiler_params=_params("arbitrary"),
        name="post_norm_modulate",
    )(xp, yp, mp, xs, ys, ms, g.reshape(1, d), b.reshape(1, d), mpn, msn)


def _router_kernel(h_ref, w_ref, b_ref, idx_ref, gate_ref):
    logits = jnp.dot(h_ref[...].astype(BF16), w_ref[...].astype(BF16),
                     preferred_element_type=F32) + b_ref[...]
    n_exp = logits.shape[-1]
    lane = lax.broadcasted_iota(jnp.int32, logits.shape, 1)
    vals = []
    for k in range(TOP_K):
        m = jnp.max(logits, axis=-1, keepdims=True)
        sel = jnp.min(jnp.where(logits == m, lane, n_exp), axis=-1, keepdims=True)
        idx_ref[:, k:k + 1] = sel
        vals.append(m)
        logits = jnp.where(lane == sel, -jnp.inf, logits)
    exps = [jnp.exp(v - vals[0]) for v in vals]
    denom = exps[0]
    for e in exps[1:]:
        denom = denom + e
    for k in range(TOP_K):
        gate_ref[:, k:k + 1] = exps[k] / denom


def _router(h, w_r, b_r):
    t, d = h.shape
    e = w_r.shape[-1]
    tt = _tile(t, 512)
    return pl.pallas_call(
        _router_kernel,
        grid=(t // tt,),
        in_specs=[pl.BlockSpec((tt, d), lambda i: (i, 0)),
                  pl.BlockSpec((d, e), lambda i: (0, 0)),
                  pl.BlockSpec((1, e), lambda i: (0, 0))],
        out_specs=[pl.BlockSpec((tt, TOP_K), lambda i: (i, 0)),
                   pl.BlockSpec((tt, TOP_K), lambda i: (i, 0))],
        out_shape=[jax.ShapeDtypeStruct((t, TOP_K), jnp.int32),
                   jax.ShapeDtypeStruct((t, TOP_K), F32)],
        compiler_params=_params("arbitrary"),
        name="router_topk",
    )(h, w_r, b_r.reshape(1, e))


def _row_gather(idx_smem, src_hbm, dst, sem, n, base=0):
    def body(r, carry):
        pltpu.make_async_copy(src_hbm.at[pl.ds(idx_smem[base + r], 1)], dst.at[pl.ds(r, 1)], sem).start()
        return carry
    lax.fori_loop(0, n, body, 0)


def _dispatch_kernel(nu_ref, tok_hbm, h_hbm, o_ref, tok_smem, buf, sem_i, sem_g, *, tm):
    i = pl.program_id(0)

    @pl.when(i < nu_ref[0])
    def _():
        cp = pltpu.make_async_copy(tok_hbm.at[i], tok_smem, sem_i)
        cp.start()
        cp.wait()
        _row_gather(tok_smem, h_hbm, buf, sem_g, tm)
        pltpu.make_async_copy(h_hbm.at[pl.ds(0, tm)], buf, sem_g).wait()
        o_ref[...] = buf[...].astype(o_ref.dtype)

    @pl.when(i >= nu_ref[0])
    def _():
        o_ref[...] = jnp.zeros_like(o_ref)


def _dispatch(h, row_tok, n_used, tm):
    t, d = h.shape
    n_blocks = row_tok.shape[0]
    return pl.pallas_call(
        functools.partial(_dispatch_kernel, tm=tm),
        grid_spec=pltpu.PrefetchScalarGridSpec(
            num_scalar_prefetch=1,
            grid=(n_blocks,),
            in_specs=[pl.BlockSpec(memory_space=pl.ANY), pl.BlockSpec(memory_space=pl.ANY)],
            out_specs=pl.BlockSpec((tm, d), lambda i, nu: (i, 0)),
            scratch_shapes=[pltpu.SMEM((tm,), jnp.int32), pltpu.VMEM((tm, d), F32),
                            pltpu.SemaphoreType.DMA, pltpu.SemaphoreType.DMA]),
        out_shape=jax.ShapeDtypeStruct((n_blocks * tm, d), BF16),
        compiler_params=_params("arbitrary"),
        name="moe_dispatch",
    )(n_used, row_tok, h)


def _deinterleave_matrix():
    g = V7X_MXU_DIM
    r = lax.broadcasted_iota(jnp.int32, (g, g), 0)
    c = lax.broadcasted_iota(jnp.int32, (g, g), 1)
    src = jnp.where(c < g // 2, 2 * c, 2 * (c - g // 2) + 1)
    return (r == src).astype(BF16)


def _ffn1_kernel(be_ref, first_ref, nu_ref, x_ref, w_ref, b_ref, o_ref, wb_ref, *, tn):
    i = pl.program_id(1)
    valid = i < nu_ref[0]
    grp = V7X_MXU_DIM
    half = grp // 2

    @pl.when(valid & (first_ref[i] == 1))
    def _():
        perm = _deinterleave_matrix()
        for g in range(tn // grp):
            cols = slice(g * grp, (g + 1) * grp)
            wg = w_ref[:, cols].astype(BF16)
            wb_ref[:, cols] = jnp.dot(wg, perm, preferred_element_type=F32).astype(BF16)

    @pl.when(valid)
    def _():
        hg = jnp.dot(x_ref[...], wb_ref[...], preferred_element_type=F32) + b_ref[...]
        for g in range(tn // grp):
            glu = jnp.minimum(hg[:, g * grp:g * grp + half], SWIGLU_LIMIT)
            lin = jnp.clip(hg[:, g * grp + half:(g + 1) * grp], -SWIGLU_LIMIT, SWIGLU_LIMIT)
            act = glu * jax.nn.sigmoid(SWIGLU_ALPHA * glu) * (lin + 1.0)
            o_ref[:, g * half:(g + 1) * half] = act.astype(o_ref.dtype)

    @pl.when(jnp.logical_not(valid))
    def _():
        o_ref[...] = jnp.zeros_like(o_ref)


def _ffn2_kernel(be_ref, first_ref, nu_ref, a_ref, w_ref, b_ref, o_ref, wb_ref):
    i = pl.program_id(1)
    valid = i < nu_ref[0]

    @pl.when(valid & (first_ref[i] == 1))
    def _():
        wb_ref[...] = w_ref[...].astype(BF16)

    @pl.when(valid)
    def _():
        o_ref[...] = jnp.dot(a_ref[...], wb_ref[...], preferred_element_type=F32) + b_ref[...]

    @pl.when(jnp.logical_not(valid))
    def _():
        o_ref[...] = jnp.zeros_like(o_ref)


def _grouped_mm(kernel, x, w, b, block_e, is_first, n_used, *, tm, tn, n_out, out_dtype, name):
    n_rows, k = x.shape
    n_blocks = n_rows // tm
    n = w.shape[-1]
    last = lambda nu: jnp.maximum(nu[0] - 1, 0)
    x_map = lambda j, i, be, fi, nu: (jnp.minimum(i, last(nu)), 0)
    w_map = lambda j, i, be, fi, nu: (be[jnp.minimum(i, last(nu))], 0, j)
    o_map = lambda j, i, be, fi, nu: (i, j)
    return pl.pallas_call(
        kernel,
        grid_spec=pltpu.PrefetchScalarGridSpec(
            num_scalar_prefetch=3,
            grid=(n // tn, n_blocks),
            in_specs=[pl.BlockSpec((tm, k), x_map),
                      pl.BlockSpec((None, k, tn), w_map),
                      pl.BlockSpec((None, 1, tn), w_map)],
            out_specs=pl.BlockSpec((tm, n_out // (n // tn)), o_map),
            scratch_shapes=[pltpu.VMEM((k, tn), BF16)]),
        out_shape=jax.ShapeDtypeStruct((n_rows, n_out), out_dtype),
        compiler_params=_params("arbitrary", "arbitrary"),
        name=name,
    )(block_e, is_first, n_used, x, w, b)


def _combine_kernel(pos_hbm, y_hbm, xp_ref, xs_ref, gates_ref, mp_ref, ms_ref, g_ref, b_ref, *rest,
                    tl, n_p, bs, ls, alpha, has_next):
    if has_next:
        mpn_ref, msn_ref, xpo_ref, xso_ref, hp_ref, hs_ref, pos_smem, buf, y_buf, sem_i, sem_g = rest
    else:
        xpo_ref, xso_ref, pos_smem, buf, y_buf, sem_i, sem_g = rest
        mpn_ref = msn_ref = hp_ref = hs_ref = None
    s = pl.program_id(0)
    cp = pltpu.make_async_copy(pos_hbm.at[s], pos_smem, sem_i)
    cp.start()
    cp.wait()
    for k in range(TOP_K):
        _row_gather(pos_smem, y_hbm, buf.at[k], sem_g, tl, base=k * tl)
    for k in range(TOP_K):
        pltpu.make_async_copy(y_hbm.at[pl.ds(0, tl)], buf.at[k], sem_g).wait()
    y = gates_ref[:, 0:1] * buf[0]
    for k in range(1, TOP_K):
        y = y + gates_ref[:, k:k + 1] * buf[k]
    y_buf[...] = y

    @pl.when(s < n_p)
    def _():
        _norm_mod_store(xp_ref[...], y_buf[...], mp_ref[...], g_ref[...], b_ref[...],
                        mpn_ref[...] if has_next else None, xpo_ref, hp_ref, slice(None), alpha)

    @pl.when(s == n_p)
    def _():
        for q in range(bs):
            rows = slice(q * ls, (q + 1) * ls)
            _norm_mod_store(xs_ref[rows, :], y_buf[rows, :], ms_ref[q], g_ref[...], b_ref[...],
                            msn_ref[q] if has_next else None, xso_ref, hs_ref, rows, alpha)


def _combine(xp, xs, y_rows, pos, gates, mp, ms, g, b, mpn, msn, *, l_p, alpha, h_dtype):
    t_p, d = xp.shape
    bs = ms.shape[0]
    ls = xs.shape[0] // bs
    tl, n_p, p_blk, p_mod = _tile_plan(t_p, l_p, bs, ls)
    has_next = mpn is not None
    pos_tiles = pos.reshape(n_p + 1, tl, TOP_K).transpose(0, 2, 1).reshape(n_p + 1, TOP_K * tl)
    whole2 = lambda s: (0, 0)
    whole3 = lambda s: (0, 0, 0)
    in_specs = [pl.BlockSpec(memory_space=pl.ANY), pl.BlockSpec(memory_space=pl.ANY),
                pl.BlockSpec((tl, d), p_blk), pl.BlockSpec((tl, d), whole2),
                pl.BlockSpec((tl, TOP_K), lambda s: (s, 0)),
                pl.BlockSpec((None, 3, d), p_mod), pl.BlockSpec((bs, 3, d), whole3),
                pl.BlockSpec((1, d), whole2), pl.BlockSpec((1, d), whole2)]
    args = [pos_tiles, y_rows, xp, xs, gates, mp, ms, g.reshape(1, d), b.reshape(1, d)]
    out_specs = [pl.BlockSpec((tl, d), p_blk), pl.BlockSpec((tl, d), whole2)]
    out_shape = [jax.ShapeDtypeStruct((t_p, d), F32), jax.ShapeDtypeStruct((tl, d), F32)]
    if has_next:
        in_specs += [pl.BlockSpec((None, 3, d), p_mod), pl.BlockSpec((bs, 3, d), whole3)]
        args += [mpn, msn]
        out_specs += [pl.BlockSpec((tl, d), p_blk), pl.BlockSpec((tl, d), whole2)]
        out_shape += [jax.ShapeDtypeStruct((t_p, d), h_dtype), jax.ShapeDtypeStruct((tl, d), h_dtype)]
    res = pl.pallas_call(
        functools.partial(_combine_kernel, tl=tl, n_p=n_p, bs=bs, ls=ls, alpha=alpha, has_next=has_next),
        grid=(n_p + 1,),
        in_specs=in_specs,
        out_specs=out_specs,
        out_shape=out_shape,
        scratch_shapes=[pltpu.SMEM((TOP_K * tl,), jnp.int32), pltpu.VMEM((TOP_K, tl, d), F32),
                        pltpu.VMEM((tl, d), F32), pltpu.SemaphoreType.DMA, pltpu.SemaphoreType.DMA],
        compiler_params=_params("arbitrary"),
        name="moe_combine_post_norm",
    )(*args)
    return res if has_next else (res[0], res[1], None, None)


def _moe_plan(idx, n_experts, tm):
    t = idx.shape[0]
    n_assign = t * TOP_K
    flat_e = idx.reshape(-1)
    order = jnp.argsort(flat_e).astype(jnp.int32)
    counts = jnp.sum((flat_e[:, None] == jnp.arange(n_experts, dtype=jnp.int32)[None, :]).astype(jnp.int32), axis=0)
    padded = (counts + tm - 1) // tm * tm
    start = jnp.cumsum(counts) - counts
    pad_end = jnp.cumsum(padded)
    pad_start = pad_end - padded
    n_blocks = -(-(n_assign + n_experts * (tm - 1)) // tm)
    n_used = (pad_end[-1] // tm).astype(jnp.int32).reshape(1)
    blk = jnp.arange(n_blocks, dtype=jnp.int32)
    block_e = jnp.minimum(jnp.searchsorted(pad_end, blk * tm, side="right"), n_experts - 1).astype(jnp.int32)
    is_first = jnp.concatenate([jnp.ones((1,), jnp.int32),
                                (block_e[1:] != block_e[:-1]).astype(jnp.int32)])
    r = jnp.arange(n_blocks * tm, dtype=jnp.int32)
    e_r = block_e[r // tm]
    off = r - pad_start[e_r]
    s = jnp.clip(start[e_r] + off, 0, n_assign - 1)
    row_tok = jnp.where(off < counts[e_r], order[s] // TOP_K, 0).astype(jnp.int32)
    e_sorted = flat_e[order]
    dest = (pad_start[e_sorted] + jnp.arange(n_assign, dtype=jnp.int32) - start[e_sorted]).astype(jnp.int32)
    _, pos = lax.sort_key_val(order, dest)
    return row_tok.reshape(n_blocks, tm), pos.reshape(t, TOP_K), block_e, is_first, n_used


def _moe_rows(h_all, w_r, b_r, w1, b1, w2, b2):
    t, d = h_all.shape
    n_experts = w_r.shape[-1]
    d_e2 = w1.shape[-1]
    tm = _tile(MOE_ROWS, MOE_ROWS)
    idx, gates = _router(h_all, w_r, b_r)
    row_tok, pos, block_e, is_first, n_used = _moe_plan(idx, n_experts, tm)
    xs = _dispatch(h_all, row_tok, n_used, tm)
    grp = V7X_MXU_DIM
    b1p = b1.reshape(n_experts, d_e2 // grp, grp // 2, 2).transpose(0, 1, 3, 2).reshape(n_experts, 1, d_e2)
    tn1 = _tile(d_e2, 512, grp)
    act = _grouped_mm(functools.partial(_ffn1_kernel, tn=tn1), xs, w1, b1p, block_e, is_first, n_used,
                      tm=tm, tn=tn1, n_out=d_e2 // 2, out_dtype=BF16, name="moe_ffn1")
    tn2 = _tile(d, 1024, 128)
    y_rows = _grouped_mm(_ffn2_kernel, act, w2, b2.reshape(n_experts, 1, d), block_e, is_first, n_used,
                         tm=tm, tn=tn2, n_out=d, out_dtype=F32, name="moe_ffn2")
    return y_rows, pos, gates


def _after_matrix(n):
    return (lax.broadcasted_iota(jnp.int32, (n, n), 0) > lax.broadcasted_iota(jnp.int32, (n, n), 1)).astype(BF16)


def _sb_block(q, qpos, kblk, vblk, kpos0, acc, c, scale):
    tk = kblk.shape[0]
    z = lax.dot_general(q, kblk.astype(BF16), (((1,), (1,)), ((), ())),
                        preferred_element_type=F32) * scale
    kpos = kpos0 + lax.broadcasted_iota(jnp.int32, (1, tk), 1)
    mask = kpos < qpos
    sp = jnp.maximum(z, 0.0) + jnp.log1p(jnp.exp(-jnp.abs(z)))
    lk = jnp.where(mask, -sp, 0.0)
    hi = lk.astype(BF16)
    r1 = lk - hi.astype(F32)
    mid = r1.astype(BF16)
    lo = (r1 - mid.astype(F32)).astype(BF16)
    u = _after_matrix(tk)
    after = (jnp.dot(hi, u, preferred_element_type=F32) + jnp.dot(mid, u, preferred_element_type=F32)
             + jnp.dot(lo, u, preferred_element_type=F32)) + c
    w = jnp.where(mask, jnp.exp(z - sp + after), 0.0)
    acc = acc + jnp.dot(w.astype(BF16), vblk.astype(BF16), preferred_element_type=F32)
    c = c + jnp.sum(lk, axis=-1, keepdims=True)
    return acc, c


def _attn_kernel(*refs, tq, tk_cache, q_start, scale, has_cache):
    if has_cache:
        q_ref, k_ref, v_ref, ck_ref, cv_ref, o_ref = refs
    else:
        q_ref, k_ref, v_ref, o_ref = refs
    l = q_ref.shape[0]
    dh = q_ref.shape[1]
    past = ck_ref.shape[0] if has_cache else 0

    def q_block(qi, carry):
        r0 = pl.multiple_of(qi * tq, tq)
        q = q_ref[pl.ds(r0, tq), :]
        qpos = q_start + r0 + lax.broadcasted_iota(jnp.int32, (tq, 1), 0)

        def alive(c):
            return jnp.max(c) > ATTN_DEAD_LOG

        def new_cond(s):
            kb, _, c = s
            return (kb >= 0) & alive(c)

        def new_body(s):
            kb, acc, c = s
            k0 = pl.multiple_of(kb * tq, tq)
            acc, c = _sb_block(q, qpos, k_ref[pl.ds(k0, tq), :], v_ref[pl.ds(k0, tq), :],
                               past + k0, acc, c, scale)
            return kb - 1, acc, c

        state = (qi, jnp.zeros((tq, dh), F32), jnp.zeros((tq, 1), F32))
        _, acc, c = lax.while_loop(new_cond, new_body, state)

        if has_cache:
            def cache_body(s):
                kb, acc, c = s
                k0 = pl.multiple_of(kb * tk_cache, tk_cache)
                acc, c = _sb_block(q, qpos, ck_ref[pl.ds(k0, tk_cache), :], cv_ref[pl.ds(k0, tk_cache), :],
                                   k0, acc, c, scale)
                return kb - 1, acc, c

            _, acc, c = lax.while_loop(new_cond, cache_body, (past // tk_cache - 1, acc, c))

        o_ref[pl.ds(r0, tq), :] = acc.astype(o_ref.dtype)
        return carry

    lax.fori_loop(0, l // tq, q_block, 0)


def _attention(q, k, v, cache_k, cache_v, *, bn, n_heads):
    m, d = q.shape
    l = m // bn
    dh = d // n_heads
    has_cache = cache_k is not None
    past = cache_k.shape[0] // bn if has_cache else 0
    tq = _tile(l, 256)
    tk_cache = _tile(past, 256) if has_cache else 0
    blk = lambda b, h: (b, h)
    in_specs = [pl.BlockSpec((l, dh), blk)] * 3
    args = [q, k, v]
    if has_cache:
        in_specs += [pl.BlockSpec((past, dh), blk)] * 2
        args += [cache_k, cache_v]
    return pl.pallas_call(
        functools.partial(_attn_kernel, tq=tq, tk_cache=tk_cache, q_start=past, scale=dh ** -0.5,
                          has_cache=has_cache),
        grid=(bn, n_heads),
        in_specs=in_specs,
        out_specs=pl.BlockSpec((l, dh), blk),
        out_shape=jax.ShapeDtypeStruct((m, d), BF16),
        compiler_params=_params("arbitrary", "arbitrary"),
        name="stick_breaking_attention",
    )(*args)


def kernel(x_prompt, x_sample, state_pool, cache_k, cache_v, c_prompt, c_sample, ada_w, ada_b, ln_g, ln_b,
           even_w_in, gmlp_ln_g, gmlp_ln_b, gmlp_ws, gmlp_b, pool_w, pool_scale, even_w_out, odd_w_qkv,
           odd_w_o, router_w, router_b, moe_w1, moe_b1, moe_w2, moe_b2):
    depth = ada_w.shape[0]
    d = x_prompt.shape[-1]
    n_heads = cache_k.shape[-2]
    d_a = gmlp_ln_g.shape[1] * gmlp_ln_g.shape[2]
    d_b = pool_scale.shape[-1]
    alpha = (2 * depth) ** 0.25
    nb_p = x_prompt.shape[0]
    trunks = [dict(x=x_prompt, b0=0, pos0=0, pool=None, ck=None, cv=None),
              dict(x=x_sample, b0=nb_p, pos0=cache_k.shape[2], pool=state_pool, ck=cache_k, cv=cache_v)]
    tp, ts = trunks
    for tr in trunks:
        tr["bn"], tr["l"] = tr["x"].shape[0], tr["x"].shape[1]
        tr["pools"], tr["gvs"], tr["ks"], tr["vs"] = [], [], [], []

    mods = _ada(jnp.concatenate([c_prompt, c_sample], axis=0), ada_w, ada_b)
    mods = mods.reshape(depth, 2, mods.shape[1], 3, d)

    def mod_of(tr, layer, sub):
        return mods[layer, sub, tr["b0"]:tr["b0"] + tr["bn"]]

    for tr in trunks:
        tr["h"] = _modulate(tr["x"], mod_of(tr, 0, 0), BF16).reshape(-1, d)
        tr["x"] = tr["x"].reshape(-1, d)

    for layer in range(depth):
        j = layer // 2
        for tr in trunks:
            bn, l = tr["bn"], tr["l"]
            h2 = tr["h"]
            if layer % 2 == 0:
                w_in = even_w_in[j]
                u = _mm(h2, w_in, 0, d_a, act="gelu", name="even_in_u").reshape(bn, l, d_a)
                v = _mm(h2, w_in, d_a, d_a, act="gelu", name="even_in_v").reshape(bn, l, d_a)
                p = _mm(h2, w_in, 2 * d_a, d_b, name="even_in_p").reshape(bn, l, d_b)
                n_hist = POOL_HALO - 1
                hist = jnp.zeros((bn, n_hist, d_b), F32) if tr["pool"] is None else tr["pool"][j]
                ab, gv = _even_mix(u, v, p, hist, gmlp_ln_g[j], gmlp_ln_b[j], gmlp_ws[j], gmlp_b[j],
                                   pool_w[j], pool_scale[j], pos0=tr["pos0"], emit_v=tr["pool"] is not None)
                tr["pools"].append(p[:, l - n_hist:] if l >= n_hist
                                   else jnp.concatenate([hist, p], axis=1)[:, -n_hist:])
                tr["gvs"].append(gv)
                y = _mm(ab.reshape(bn * l, d_a + d_b), even_w_out[j], 0, d, name="even_out")
            else:
                w_qkv = odd_w_qkv[j]
                q = _mm(h2, w_qkv, 0, d, out_dtype=BF16, name="attn_q")
                k = _mm(h2, w_qkv, d, d, name="attn_k")
                v = _mm(h2, w_qkv, 2 * d, d, name="attn_v")
                ck = None if tr["ck"] is None else tr["ck"][j].reshape(-1, d)
                cv = None if tr["cv"] is None else tr["cv"][j].reshape(-1, d)
                o = _attention(q, k, v, ck, cv, bn=bn, n_heads=n_heads)
                tr["ks"].append(k.reshape(bn, l, n_heads, d // n_heads))
                tr["vs"].append(v.reshape(bn, l, n_heads, d // n_heads))
                y = _mm(o, odd_w_o[j], 0, d, name="attn_out")
            tr["y"] = y
        tp["x"], ts["x"], h_all = _lnres(
            tp["x"], tp["y"], mod_of(tp, layer, 0), ts["x"], ts["y"], mod_of(ts, layer, 0),
            ln_g[layer, 0], ln_b[layer, 0], mod_of(tp, layer, 1), mod_of(ts, layer, 1),
            l_p=tp["l"], alpha=alpha, h_dtype=F32)
        y_rows, pos, gates = _moe_rows(h_all, router_w[layer], router_b[layer], moe_w1[layer], moe_b1[layer],
                                       moe_w2[layer], moe_b2[layer])
        last = layer + 1 == depth
        tp["x"], ts["x"], tp["h"], ts["h"] = _combine(
            tp["x"], ts["x"], y_rows, pos, gates, mod_of(tp, layer, 1), mod_of(ts, layer, 1),
            ln_g[layer, 1], ln_b[layer, 1], None if last else mod_of(tp, layer + 1, 0),
            None if last else mod_of(ts, layer + 1, 0), l_p=tp["l"], alpha=alpha, h_dtype=BF16)

    for tr in trunks:
        tr["x"] = tr["x"].reshape(tr["bn"], tr["l"], d)
    return (tp["x"], ts["x"], jnp.stack(tp["pools"]), jnp.stack(ts["pools"]), jnp.stack(ts["gvs"]),
            jnp.stack(tp["ks"]), jnp.stack(tp["vs"]), jnp.stack(ts["ks"]), jnp.stack(ts["vs"]))
```

```python
import functools

import jax
import jax.numpy as jnp
from jax import lax
from jax.experimental import pallas as pl
from jax.experimental.pallas import tpu as pltpu

F32 = jnp.float32
BF16 = jnp.bfloat16

POOL_WINDOWS = (2, 4, 8, 16)
POOL_HALO = 16
TOP_K = 4
GMLP_CHUNK = 128
SWIGLU_ALPHA = 1.702
SWIGLU_LIMIT = 7.0
LN_EPS = 1e-5

V7X_VMEM_LIMIT_BYTES = 56 * 1024 * 1024
V7X_MXU_DIM = 256

MOE_ROWS = 512
FFN1_COLS = 1024
FFN2_COLS = 2048
ATTN_HEADS_PER_STEP = 2
ATTN_DEAD_LOG = -110.0


def _tile(n, pref, mult=8):
    if n <= pref:
        return n
    t = (pref // mult) * mult
    while t >= mult:
        if n % t == 0:
            return t
        t -= mult
    return n


def _params(*sem):
    return pltpu.CompilerParams(dimension_semantics=sem, vmem_limit_bytes=V7X_VMEM_LIMIT_BYTES)


def _layer_norm_rows(t, g, b):
    mu = jnp.mean(t, axis=-1, keepdims=True)
    tc = t - mu
    var = jnp.mean(tc * tc, axis=-1, keepdims=True)
    return tc * lax.rsqrt(var + LN_EPS) * g + b


def _ada_kernel(c_ref, w_ref, b_ref, o_ref):
    c = c_ref[...]
    s = (c * jax.nn.sigmoid(c)).astype(BF16)
    o_ref[...] = jnp.dot(s, w_ref[...].astype(BF16), preferred_element_type=F32) + b_ref[...]


def _ada(c_all, ada_w, ada_b):
    nb, d = c_all.shape
    n = ada_w.shape[-1]
    w = ada_w.reshape(-1, d, n)
    b = ada_b.reshape(-1, 1, n)
    npair = w.shape[0]
    tn = _tile(n, 512, 128)
    return pl.pallas_call(
        _ada_kernel,
        grid=(npair, n // tn),
        in_specs=[pl.BlockSpec((nb, d), lambda p, j: (0, 0)),
                  pl.BlockSpec((None, d, tn), lambda p, j: (p, 0, j)),
                  pl.BlockSpec((None, 1, tn), lambda p, j: (p, 0, j))],
        out_specs=pl.BlockSpec((None, nb, tn), lambda p, j: (p, 0, j)),
        out_shape=jax.ShapeDtypeStruct((npair, nb, n), F32),
        compiler_params=_params("arbitrary", "arbitrary"),
        name="ada_modulation",
    )(c_all, w, b)


def _mod_kernel(x_ref, m_ref, h_ref):
    h_ref[...] = (x_ref[...] * (1.0 + m_ref[1:2, :]) + m_ref[0:1, :]).astype(h_ref.dtype)


def _modulate(x, mod, out_dtype):
    bn, l, d = x.shape
    tl = _tile(l, 256)
    return pl.pallas_call(
        _mod_kernel,
        grid=(bn, l // tl),
        in_specs=[pl.BlockSpec((None, tl, d), lambda b, i: (b, i, 0)),
                  pl.BlockSpec((None, 3, d), lambda b, i: (b, 0, 0))],
        out_specs=pl.BlockSpec((None, tl, d), lambda b, i: (b, i, 0)),
        out_shape=jax.ShapeDtypeStruct((bn, l, d), out_dtype),
        compiler_params=_params("arbitrary", "arbitrary"),
        name="modulate",
    )(x, mod)


def _mm_kernel(a_ref, w_ref, o_ref, wb_ref, *, act):
    @pl.when(pl.program_id(1) == 0)
    def _():
        wb_ref[...] = w_ref[...].astype(BF16)

    acc = jnp.dot(a_ref[...], wb_ref[...], preferred_element_type=F32)
    if act == "gelu":
        acc = jax.nn.gelu(acc, approximate=True)
    o_ref[...] = acc.astype(o_ref.dtype)


def _mm(a, w, col0, n, *, act=None, out_dtype=F32, name="matmul"):
    m, k = a.shape
    tm = _tile(m, 1024)
    tn = _tile(n, 512, 128)
    assert col0 % tn == 0
    joff = col0 // tn
    return pl.pallas_call(
        functools.partial(_mm_kernel, act=act),
        grid=(n // tn, m // tm),
        in_specs=[pl.BlockSpec((tm, k), lambda j, i: (i, 0)),
                  pl.BlockSpec((k, tn), lambda j, i: (0, j + joff))],
        out_specs=pl.BlockSpec((tm, tn), lambda j, i: (i, j)),
        out_shape=jax.ShapeDtypeStruct((m, n), out_dtype),
        scratch_shapes=[pltpu.VMEM((k, tn), BF16)],
        compiler_params=_params("arbitrary", "arbitrary"),
        name=name,
    )(a, w)


def _even_kernel(u_ref, v_ref, p_ref, pprev_ref, hist_ref, glng_ref, glnb_ref, ws_ref, gbt_ref,
                 pw_ref, ps_ref, *rest, tl, chunk, n_groups, pos0, emit_v):
    if emit_v:
        ab_ref, gv_ref, pwb_ref = rest
    else:
        ab_ref, pwb_ref = rest
        gv_ref = None
    b = pl.program_id(0)
    i = pl.program_id(1)
    d_a = u_ref.shape[-1]
    d_b = p_ref.shape[-1]
    ag = d_a // n_groups
    bg = d_b // len(POOL_WINDOWS)

    @pl.when((b == 0) & (i == 0))
    def _():
        pwb_ref[...] = pw_ref[...].astype(BF16)

    tri = (lax.broadcasted_iota(jnp.int32, (chunk, chunk), 0)
           >= lax.broadcasted_iota(jnp.int32, (chunk, chunk), 1))
    for g in range(n_groups):
        sl = slice(g * ag, (g + 1) * ag)
        vn = _layer_norm_rows(v_ref[:, sl], glng_ref[:, sl], glnb_ref[:, sl])
        if emit_v:
            gv_ref[:, sl] = vn
        wm = jnp.where(tri, ws_ref[g][:chunk, :chunk], 0.0).astype(BF16)
        vb = vn.astype(BF16)
        bias = gbt_ref[:chunk, g:g + 1]
        for c in range(tl // chunk):
            rows = slice(c * chunk, (c + 1) * chunk)
            mixed = jnp.dot(wm, vb[rows], preferred_element_type=F32) + bias
            ab_ref[rows, sl] = (u_ref[rows, sl] * mixed).astype(ab_ref.dtype)

    prev = jnp.where(i == 0, hist_ref[...], pprev_ref[...])
    p = p_ref[...]
    ext = jnp.concatenate([prev, p], axis=0)
    pos = pos0 + i * tl + lax.broadcasted_iota(jnp.int32, (tl, 1), 0)
    for gi, w in enumerate(POOL_WINDOWS):
        sl = slice(gi * bg, (gi + 1) * bg)
        s = ext[:, sl]
        step = 1
        while step < w:
            s = s + pltpu.roll(s, step, 0)
            step *= 2
        cnt = jnp.minimum(w, pos + 1).astype(F32)
        pooled = s[POOL_HALO:, :] / cnt - p[:, sl]
        mixed = jnp.dot(pooled.astype(BF16), pwb_ref[gi], preferred_element_type=F32) * ps_ref[:, sl]
        ab_ref[:, d_a + gi * bg:d_a + (gi + 1) * bg] = mixed.astype(ab_ref.dtype)


def _even_mix(u, v, p, hist, gln_g, gln_b, gws, gb, pool_w, pool_scale, *, pos0, emit_v):
    bn, l, d_a = u.shape
    d_b = p.shape[-1]
    n_groups = gws.shape[0]
    chunk = min(l, GMLP_CHUNK)
    tl = _tile(l, 2 * GMLP_CHUNK, chunk)
    assert l % chunk == 0 and tl % chunk == 0 and tl % POOL_HALO == 0
    halo_blocks = tl // POOL_HALO
    hist16 = jnp.concatenate(
        [jnp.zeros((bn, POOL_HALO - hist.shape[1], d_b), F32), hist], axis=1)
    out_shape = [jax.ShapeDtypeStruct((bn, l, d_a + d_b), BF16)]
    out_specs = [pl.BlockSpec((None, tl, d_a + d_b), lambda b, i: (b, i, 0))]
    if emit_v:
        out_shape.append(jax.ShapeDtypeStruct((bn, l, d_a), F32))
        out_specs.append(pl.BlockSpec((None, tl, d_a), lambda b, i: (b, i, 0)))
    row_blk = lambda b, i: (b, i, 0)
    whole2 = lambda b, i: (0, 0)
    whole3 = lambda b, i: (0, 0, 0)
    res = pl.pallas_call(
        functools.partial(_even_kernel, tl=tl, chunk=chunk, n_groups=n_groups, pos0=pos0, emit_v=emit_v),
        grid=(bn, l // tl),
        in_specs=[pl.BlockSpec((None, tl, d_a), row_blk),
                  pl.BlockSpec((None, tl, d_a), row_blk),
                  pl.BlockSpec((None, tl, d_b), row_blk),
                  pl.BlockSpec((None, POOL_HALO, d_b),
                               lambda b, i: (b, jnp.maximum(i * halo_blocks - 1, 0), 0)),
                  pl.BlockSpec((None, POOL_HALO, d_b), lambda b, i: (b, 0, 0)),
                  pl.BlockSpec((1, d_a), whole2),
                  pl.BlockSpec((1, d_a), whole2),
                  pl.BlockSpec(gws.shape, whole3),
                  pl.BlockSpec((gb.shape[1], gb.shape[0]), whole2),
                  pl.BlockSpec(pool_w.shape, whole3),
                  pl.BlockSpec((1, d_b), whole2)],
        out_specs=out_specs,
        out_shape=out_shape,
        scratch_shapes=[pltpu.VMEM(pool_w.shape, BF16)],
        compiler_params=_params("arbitrary", "arbitrary"),
        name="even_mixer",
    )(u, v, p, p, hist16, gln_g.reshape(1, d_a), gln_b.reshape(1, d_a), gws, gb.T, pool_w,
      pool_scale.reshape(1, d_b))
    return res if emit_v else (res[0], None)


def _post_norm(x, y, gate, g, b, alpha):
    return _layer_norm_rows(alpha * x + (1.0 + gate) * y, g, b)


def _pack_bf16_pairs(h):
    half = h.shape[-1] // 2
    bits = lax.bitcast_convert_type(h.astype(BF16).astype(F32), jnp.uint32)
    return (bits[:, :half] >> 16) | (bits[:, half:] & jnp.uint32(0xFFFF0000))


def _unpack_bf16_pairs(p):
    lo = lax.bitcast_convert_type(p << 16, F32)
    hi = lax.bitcast_convert_type(p & jnp.uint32(0xFFFF0000), F32)
    return jnp.concatenate([lo, hi], axis=-1).astype(BF16)


def _norm_mod_store(x, y, m, g, b, mn, xo_ref, h_ref, rows, alpha):
    xn = _post_norm(x, y, m[2:3, :], g, b, alpha)
    xo_ref[rows, :] = xn
    if h_ref is not None:
        h = xn * (1.0 + mn[1:2, :]) + mn[0:1, :]
        h_ref[rows, :] = _pack_bf16_pairs(h) if h_ref.dtype == jnp.uint32 else h.astype(h_ref.dtype)


def _tile_plan(t_p, l_p, bs, ls):
    tl = bs * ls
    assert l_p % tl == 0 and tl % 8 == 0
    n_p = t_p // tl
    per_stream = l_p // tl
    p_blk = lambda s: (jnp.minimum(s, n_p - 1), 0)
    p_mod = lambda s: (jnp.minimum(s, n_p - 1) // per_stream, 0, 0)
    return tl, n_p, p_blk, p_mod


def _lnres_kernel(xp_ref, yp_ref, mp_ref, xs_ref, ys_ref, ms_ref, g_ref, b_ref, mpn_ref, msn_ref,
                  xpo_ref, xso_ref, h_ref, *, n_p, bs, ls, alpha):
    s = pl.program_id(0)

    @pl.when(s < n_p)
    def _():
        _norm_mod_store(xp_ref[...], yp_ref[...], mp_ref[...], g_ref[...], b_ref[...], mpn_ref[...],
                        xpo_ref, h_ref, slice(None), alpha)

    @pl.when(s == n_p)
    def _():
        for q in range(bs):
            rows = slice(q * ls, (q + 1) * ls)
            _norm_mod_store(xs_ref[rows, :], ys_ref[rows, :], ms_ref[q], g_ref[...], b_ref[...], msn_ref[q],
                            xso_ref, h_ref, rows, alpha)


def _lnres(xp, yp, mp, xs, ys, ms, g, b, mpn, msn, *, l_p, alpha):
    t_p, d = xp.shape
    bs = ms.shape[0]
    ls = xs.shape[0] // bs
    tl, n_p, p_blk, p_mod = _tile_plan(t_p, l_p, bs, ls)
    whole2 = lambda s: (0, 0)
    whole3 = lambda s: (0, 0, 0)
    return pl.pallas_call(
        functools.partial(_lnres_kernel, n_p=n_p, bs=bs, ls=ls, alpha=alpha),
        grid=(n_p + 1,),
        in_specs=[pl.BlockSpec((tl, d), p_blk), pl.BlockSpec((tl, d), p_blk),
                  pl.BlockSpec((None, 3, d), p_mod),
                  pl.BlockSpec((tl, d), whole2), pl.BlockSpec((tl, d), whole2),
                  pl.BlockSpec((bs, 3, d), whole3),
                  pl.BlockSpec((1, d), whole2), pl.BlockSpec((1, d), whole2),
                  pl.BlockSpec((None, 3, d), p_mod), pl.BlockSpec((bs, 3, d), whole3)],
        out_specs=[pl.BlockSpec((tl, d), p_blk), pl.BlockSpec((tl, d), whole2),
                   pl.BlockSpec((tl, d // 2), lambda s: (s, 0))],
        out_shape=[jax.ShapeDtypeStruct((t_p, d), F32), jax.ShapeDtypeStruct((tl, d), F32),
                   jax.ShapeDtypeStruct((t_p + tl, d // 2), jnp.uint32)],
        compiler_params=_params("arbitrary"),
        name="post_norm_modulate",
    )(xp, yp, mp, xs, ys, ms, g.reshape(1, d), b.reshape(1, d), mpn, msn)


def _router_kernel(h_ref, w_ref, b_ref, idx_ref, gate_ref):
    logits = jnp.dot(_unpack_bf16_pairs(h_ref[...]), w_ref[...].astype(BF16),
                     preferred_element_type=F32) + b_ref[...]
    n_exp = logits.shape[-1]
    lane = lax.broadcasted_iota(jnp.int32, logits.shape, 1)
    vals = []
    for k in range(TOP_K):
        m = jnp.max(logits, axis=-1, keepdims=True)
        sel = jnp.min(jnp.where(logits == m, lane, n_exp), axis=-1, keepdims=True)
        idx_ref[:, k:k + 1] = sel
        vals.append(m)
        logits = jnp.where(lane == sel, -jnp.inf, logits)
    exps = [jnp.exp(v - vals[0]) for v in vals]
    denom = exps[0]
    for e in exps[1:]:
        denom = denom + e
    for k in range(TOP_K):
        gate_ref[:, k:k + 1] = exps[k] / denom


def _router(h, w_r, b_r):
    t = h.shape[0]
    d, e = w_r.shape
    tt = _tile(t, 512)
    return pl.pallas_call(
        _router_kernel,
        grid=(t // tt,),
        in_specs=[pl.BlockSpec((tt, d // 2), lambda i: (i, 0)),
                  pl.BlockSpec((d, e), lambda i: (0, 0)),
                  pl.BlockSpec((1, e), lambda i: (0, 0))],
        out_specs=[pl.BlockSpec((tt, TOP_K), lambda i: (i, 0)),
                   pl.BlockSpec((tt, TOP_K), lambda i: (i, 0))],
        out_shape=[jax.ShapeDtypeStruct((t, TOP_K), jnp.int32),
                   jax.ShapeDtypeStruct((t, TOP_K), F32)],
        compiler_params=_params("arbitrary"),
        name="router_topk",
    )(h, w_r, b_r.reshape(1, e))


GATHER_UNROLL = 8


def _row_gather(idx_smem, src_hbm, dst, sem, n, base=0):
    def body(r, carry):
        pltpu.make_async_copy(src_hbm.at[pl.ds(idx_smem[base + r], 1)], dst.at[pl.ds(r, 1)], sem).start()
        return carry
    lax.fori_loop(0, n, body, 0, unroll=GATHER_UNROLL)


def _row_gather_wait(src_hbm, dst, sem):
    pltpu.make_async_copy(src_hbm.at[pl.ds(0, dst.shape[0])], dst, sem).wait()


def _dispatch_kernel(nu_ref, tok_hbm, h_hbm, o_ref, tok_smem, buf, sem_i, sem_g, *, tm):
    i = pl.program_id(0)

    def start_block(blk, slot):
        cp = pltpu.make_async_copy(tok_hbm.at[blk], tok_smem, sem_i)
        cp.start()
        cp.wait()
        _row_gather(tok_smem, h_hbm, buf.at[slot], sem_g.at[slot], tm)

    @pl.when(i == 0)
    def _():
        start_block(0, 0)

    @pl.when(i + 1 < nu_ref[0])
    def _():
        start_block(i + 1, (i + 1) % 2)

    @pl.when(i < nu_ref[0])
    def _():
        slot = i % 2
        _row_gather_wait(h_hbm, buf.at[slot], sem_g.at[slot])
        o_ref[...] = _unpack_bf16_pairs(buf[slot])

    @pl.when(i >= nu_ref[0])
    def _():
        o_ref[...] = jnp.zeros_like(o_ref)


def _dispatch(h, row_tok, n_used, tm):
    t, dw = h.shape
    n_blocks = row_tok.shape[0]
    return pl.pallas_call(
        functools.partial(_dispatch_kernel, tm=tm),
        grid_spec=pltpu.PrefetchScalarGridSpec(
            num_scalar_prefetch=1,
            grid=(n_blocks,),
            in_specs=[pl.BlockSpec(memory_space=pl.ANY), pl.BlockSpec(memory_space=pl.ANY)],
            out_specs=pl.BlockSpec((tm, 2 * dw), lambda i, nu: (i, 0)),
            scratch_shapes=[pltpu.SMEM((tm,), jnp.int32), pltpu.VMEM((2, tm, dw), jnp.uint32),
                            pltpu.SemaphoreType.DMA, pltpu.SemaphoreType.DMA((2,))]),
        out_shape=jax.ShapeDtypeStruct((n_blocks * tm, 2 * dw), BF16),
        compiler_params=_params("arbitrary"),
        name="moe_dispatch",
    )(n_used, row_tok, h)


def _deinterleave_matrix():
    g = V7X_MXU_DIM
    r = lax.broadcasted_iota(jnp.int32, (g, g), 0)
    c = lax.broadcasted_iota(jnp.int32, (g, g), 1)
    src = jnp.where(c < g // 2, 2 * c, 2 * (c - g // 2) + 1)
    return (r == src).astype(BF16)


def _stream_expert_weights(be_ref, first_ref, nxt_ref, nu_ref, w_hbm, stage, sem, convert, *, layer, tn):
    j = pl.program_id(0)
    i = pl.program_id(1)

    def tile_copy(e, jj):
        c0 = pl.multiple_of(jj * tn, tn)
        return pltpu.make_async_copy(w_hbm.at[layer, e, :, pl.ds(c0, tn)], stage, sem)

    @pl.when((j == 0) & (i == 0))
    def _():
        tile_copy(be_ref[0], 0).start()

    @pl.when((i < nu_ref[0]) & (first_ref[i] == 1))
    def _():
        tile_copy(be_ref[i], j).wait()
        convert()
        nxt = nxt_ref[i]

        @pl.when(nxt >= 0)
        def _():
            tile_copy(nxt, j).start()

        @pl.when((nxt < 0) & (j + 1 < pl.num_programs(0)))
        def _():
            tile_copy(be_ref[0], j + 1).start()


def _ffn1_kernel(be_ref, first_ref, nxt_ref, nu_ref, x_ref, w_hbm, b_ref, o_ref, stage, wb_ref, sem,
                 *, layer, tn):
    i = pl.program_id(1)
    valid = i < nu_ref[0]
    grp = V7X_MXU_DIM
    half = grp // 2

    def convert():
        perm = _deinterleave_matrix()
        for g in range(tn // grp):
            cols = slice(g * grp, (g + 1) * grp)
            wg = stage[:, cols].astype(BF16)
            wb_ref[:, cols] = jnp.dot(wg, perm, preferred_element_type=F32).astype(BF16)

    _stream_expert_weights(be_ref, first_ref, nxt_ref, nu_ref, w_hbm, stage, sem, convert, layer=layer, tn=tn)

    @pl.when(valid)
    def _():
        hg = jnp.dot(x_ref[...], wb_ref[...], preferred_element_type=F32) + b_ref[...]
        for g in range(tn // grp):
            glu = jnp.minimum(hg[:, g * grp:g * grp + half], SWIGLU_LIMIT)
            lin = jnp.clip(hg[:, g * grp + half:(g + 1) * grp], -SWIGLU_LIMIT, SWIGLU_LIMIT)
            act = glu * jax.nn.sigmoid(SWIGLU_ALPHA * glu) * (lin + 1.0)
            o_ref[:, g * half:(g + 1) * half] = act.astype(o_ref.dtype)

    @pl.when(jnp.logical_not(valid))
    def _():
        o_ref[...] = jnp.zeros_like(o_ref)


def _ffn2_kernel(be_ref, first_ref, nxt_ref, nu_ref, a_ref, w_hbm, b_ref, o_ref, stage, wb_ref, sem,
                 *, layer, tn):
    i = pl.program_id(1)
    valid = i < nu_ref[0]

    def convert():
        wb_ref[...] = stage[...].astype(BF16)

    _stream_expert_weights(be_ref, first_ref, nxt_ref, nu_ref, w_hbm, stage, sem, convert, layer=layer, tn=tn)

    @pl.when(valid)
    def _():
        o_ref[...] = jnp.dot(a_ref[...], wb_ref[...], preferred_element_type=F32) + b_ref[...]

    @pl.when(jnp.logical_not(valid))
    def _():
        o_ref[...] = jnp.zeros_like(o_ref)


def _grouped_mm(kernel, x, w, b, plan, *, layer, tm, tn, n_out, out_dtype, name):
    n_rows, k = x.shape
    n_blocks = n_rows // tm
    n = w.shape[-1]
    last = lambda nu: jnp.maximum(nu[0] - 1, 0)
    x_map = lambda j, i, be, fi, nx, nu: (jnp.minimum(i, last(nu)), 0)
    b_map = lambda j, i, be, fi, nx, nu: (be[jnp.minimum(i, last(nu))], 0, j)
    o_map = lambda j, i, be, fi, nx, nu: (i, j)
    return pl.pallas_call(
        functools.partial(kernel, layer=layer, tn=tn),
        grid_spec=pltpu.PrefetchScalarGridSpec(
            num_scalar_prefetch=4,
            grid=(n // tn, n_blocks),
            in_specs=[pl.BlockSpec((tm, k), x_map),
                      pl.BlockSpec(memory_space=pl.ANY),
                      pl.BlockSpec((None, 1, tn), b_map)],
            out_specs=pl.BlockSpec((tm, n_out // (n // tn)), o_map),
            scratch_shapes=[pltpu.VMEM((k, tn), F32), pltpu.VMEM((k, tn), BF16), pltpu.SemaphoreType.DMA]),
        out_shape=jax.ShapeDtypeStruct((n_rows, n_out), out_dtype),
        compiler_params=_params("arbitrary", "arbitrary"),
        name=name,
    )(plan["block_e"], plan["is_first"], plan["next_e"], plan["n_used"], x, w, b)


def _combine_kernel(pos_hbm, y_hbm, xp_ref, xs_ref, gates_ref, mp_ref, ms_ref, g_ref, b_ref, *rest,
                    tl, n_p, bs, ls, alpha, has_next):
    if has_next:
        mpn_ref, msn_ref, xpo_ref, xso_ref, hp_ref, hs_ref, pos_smem, buf, y_buf, sem_i, sem_g = rest
    else:
        xpo_ref, xso_ref, pos_smem, buf, y_buf, sem_i, sem_g = rest
        mpn_ref = msn_ref = hp_ref = hs_ref = None
    s = pl.program_id(0)

    def start_tile(t, slot):
        cp = pltpu.make_async_copy(pos_hbm.at[t], pos_smem, sem_i)
        cp.start()
        cp.wait()
        for k in range(TOP_K):
            _row_gather(pos_smem, y_hbm, buf.at[slot, k], sem_g.at[slot], tl, base=k * tl)

    @pl.when(s == 0)
    def _():
        start_tile(0, 0)

    @pl.when(s < n_p)
    def _():
        start_tile(s + 1, (s + 1) % 2)

    slot = s % 2
    for k in range(TOP_K):
        _row_gather_wait(y_hbm, buf.at[slot, k], sem_g.at[slot])
    y = gates_ref[:, 0:1] * buf[slot, 0]
    for k in range(1, TOP_K):
        y = y + gates_ref[:, k:k + 1] * buf[slot, k]
    y_buf[...] = y

    @pl.when(s < n_p)
    def _():
        _norm_mod_store(xp_ref[...], y_buf[...], mp_ref[...], g_ref[...], b_ref[...],
                        mpn_ref[...] if has_next else None, xpo_ref, hp_ref, slice(None), alpha)

    @pl.when(s == n_p)
    def _():
        for q in range(bs):
            rows = slice(q * ls, (q + 1) * ls)
            _norm_mod_store(xs_ref[rows, :], y_buf[rows, :], ms_ref[q], g_ref[...], b_ref[...],
                            msn_ref[q] if has_next else None, xso_ref, hs_ref, rows, alpha)


def _combine(xp, xs, y_rows, pos, gates, mp, ms, g, b, mpn, msn, *, l_p, alpha, h_dtype):
    t_p, d = xp.shape
    bs = ms.shape[0]
    ls = xs.shape[0] // bs
    tl, n_p, p_blk, p_mod = _tile_plan(t_p, l_p, bs, ls)
    has_next = mpn is not None
    pos_tiles = pos.reshape(n_p + 1, tl, TOP_K).transpose(0, 2, 1).reshape(n_p + 1, TOP_K * tl)
    whole2 = lambda s: (0, 0)
    whole3 = lambda s: (0, 0, 0)
    in_specs = [pl.BlockSpec(memory_space=pl.ANY), pl.BlockSpec(memory_space=pl.ANY),
                pl.BlockSpec((tl, d), p_blk), pl.BlockSpec((tl, d), whole2),
                pl.BlockSpec((tl, TOP_K), lambda s: (s, 0)),
                pl.BlockSpec((None, 3, d), p_mod), pl.BlockSpec((bs, 3, d), whole3),
                pl.BlockSpec((1, d), whole2), pl.BlockSpec((1, d), whole2)]
    args = [pos_tiles, y_rows, xp, xs, gates, mp, ms, g.reshape(1, d), b.reshape(1, d)]
    out_specs = [pl.BlockSpec((tl, d), p_blk), pl.BlockSpec((tl, d), whole2)]
    out_shape = [jax.ShapeDtypeStruct((t_p, d), F32), jax.ShapeDtypeStruct((tl, d), F32)]
    if has_next:
        in_specs += [pl.BlockSpec((None, 3, d), p_mod), pl.BlockSpec((bs, 3, d), whole3)]
        args += [mpn, msn]
        out_specs += [pl.BlockSpec((tl, d), p_blk), pl.BlockSpec((tl, d), whole2)]
        out_shape += [jax.ShapeDtypeStruct((t_p, d), h_dtype), jax.ShapeDtypeStruct((tl, d), h_dtype)]
    res = pl.pallas_call(
        functools.partial(_combine_kernel, tl=tl, n_p=n_p, bs=bs, ls=ls, alpha=alpha, has_next=has_next),
        grid=(n_p + 1,),
        in_specs=in_specs,
        out_specs=out_specs,
        out_shape=out_shape,
        scratch_shapes=[pltpu.SMEM((TOP_K * tl,), jnp.int32), pltpu.VMEM((2, TOP_K, tl, d), F32),
                        pltpu.VMEM((tl, d), F32), pltpu.SemaphoreType.DMA, pltpu.SemaphoreType.DMA((2,))],
        compiler_params=_params("arbitrary"),
        name="moe_combine_post_norm",
    )(*args)
    return res if has_next else (res[0], res[1], None, None)


def _moe_plan(idx, n_experts, tm):
    t = idx.shape[0]
    n_assign = t * TOP_K
    flat_e = idx.reshape(-1)
    order = jnp.argsort(flat_e).astype(jnp.int32)
    counts = jnp.sum((flat_e[:, None] == jnp.arange(n_experts, dtype=jnp.int32)[None, :]).astype(jnp.int32), axis=0)
    padded = (counts + tm - 1) // tm * tm
    start = jnp.cumsum(counts) - counts
    pad_end = jnp.cumsum(padded)
    pad_start = pad_end - padded
    n_blocks = -(-(n_assign + n_experts * (tm - 1)) // tm)
    n_used = (pad_end[-1] // tm).astype(jnp.int32).reshape(1)
    blk = jnp.arange(n_blocks, dtype=jnp.int32)
    block_e = jnp.minimum(jnp.sum((blk[:, None] * tm >= pad_end[None, :]).astype(jnp.int32), axis=1),
                          n_experts - 1).astype(jnp.int32)
    is_first = jnp.concatenate([jnp.ones((1,), jnp.int32),
                                (block_e[1:] != block_e[:-1]).astype(jnp.int32)])
    run_start = jnp.where((is_first == 1) & (blk < n_used[0]), blk, n_blocks)
    later_start = jnp.concatenate([lax.cummin(run_start[::-1])[::-1][1:], jnp.full((1,), n_blocks, jnp.int32)])
    next_e = jnp.where(later_start < n_blocks, block_e[jnp.minimum(later_start, n_blocks - 1)], -1).astype(jnp.int32)
    off = jnp.arange(tm, dtype=jnp.int32)[None, :] + (blk * tm - pad_start[block_e])[:, None]
    s = jnp.clip(start[block_e][:, None] + off, 0, n_assign - 1)
    row_tok = jnp.where(off < counts[block_e][:, None], order[s] // TOP_K, 0).astype(jnp.int32)
    e_sorted = flat_e[order]
    dest = (pad_start[e_sorted] + jnp.arange(n_assign, dtype=jnp.int32) - start[e_sorted]).astype(jnp.int32)
    _, pos = lax.sort_key_val(order, dest)
    plan = dict(block_e=block_e, is_first=is_first, next_e=next_e, n_used=n_used)
    return row_tok, pos.reshape(t, TOP_K), plan


def _moe_rows(h_all, layer, w_r, b_r, w1, b1, w2, b2):
    d, n_experts = w_r.shape
    d_e2 = w1.shape[-1]
    tm = MOE_ROWS
    idx, gates = _router(h_all, w_r, b_r)
    row_tok, pos, plan = _moe_plan(idx, n_experts, tm)
    xs = _dispatch(h_all, row_tok, plan["n_used"], tm)
    grp = V7X_MXU_DIM
    b1p = b1.reshape(n_experts, d_e2 // grp, grp // 2, 2).transpose(0, 1, 3, 2).reshape(n_experts, 1, d_e2)
    tn1 = _tile(d_e2, FFN1_COLS, grp)
    act = _grouped_mm(_ffn1_kernel, xs, w1, b1p, plan, layer=layer, tm=tm, tn=tn1, n_out=d_e2 // 2,
                      out_dtype=BF16, name="moe_ffn1")
    tn2 = _tile(d, FFN2_COLS, 128)
    y_rows = _grouped_mm(_ffn2_kernel, act, w2, b2.reshape(n_experts, 1, d), plan, layer=layer, tm=tm, tn=tn2,
                         n_out=d, out_dtype=F32, name="moe_ffn2")
    return y_rows, pos, gates


def _after_matrix(n):
    return (lax.broadcasted_iota(jnp.int32, (n, n), 0) > lax.broadcasted_iota(jnp.int32, (n, n), 1)).astype(BF16)


def _sb_block(q, kblk, vblk, acc, c, u, scale, mask):
    z = lax.dot_general(q, kblk.astype(BF16), (((1,), (1,)), ((), ())),
                        preferred_element_type=F32) * scale
    sp = jnp.maximum(z, 0.0) + jnp.log(1.0 + jnp.exp(-jnp.abs(z)))
    lk = -sp if mask is None else jnp.where(mask, -sp, 0.0)
    hi = lk.astype(BF16)
    lo = (lk - hi.astype(F32)).astype(BF16)
    after = jnp.dot(hi, u, preferred_element_type=F32) + jnp.dot(lo, u, preferred_element_type=F32) + c
    w = jnp.exp(z - sp + after)
    if mask is not None:
        w = jnp.where(mask, w, 0.0)
    acc = acc + jnp.dot(w.astype(BF16), vblk.astype(BF16), preferred_element_type=F32)
    c = c + jnp.sum(lk, axis=-1, keepdims=True)
    return acc, c


def _attn_kernel(*refs, tq, tk_cache, dh, scale, has_cache):
    if has_cache:
        q_ref, k_ref, v_ref, ck_ref, cv_ref, o_ref = refs
    else:
        q_ref, k_ref, v_ref, o_ref = refs
    l = q_ref.shape[0]
    heads = [slice(h * dh, (h + 1) * dh) for h in range(q_ref.shape[1] // dh)]
    past = ck_ref.shape[0] if has_cache else 0
    u_new = _after_matrix(tq)
    u_cache = _after_matrix(tk_cache) if has_cache else None
    diag_mask = (lax.broadcasted_iota(jnp.int32, (tq, tq), 1) < lax.broadcasted_iota(jnp.int32, (tq, tq), 0))

    def q_block(qi, carry):
        r0 = pl.multiple_of(qi * tq, tq)
        qs = [q_ref[pl.ds(r0, tq), hs] for hs in heads]

        def sweep(kr, vr, tk, u, k0, accs, cs, mask):
            out = [_sb_block(q, kr[pl.ds(k0, tk), hs], vr[pl.ds(k0, tk), hs], acc, c, u, scale, mask)
                   for q, hs, acc, c in zip(qs, heads, accs, cs)]
            return [o[0] for o in out], [o[1] for o in out]

        def cond(s):
            kb, _, cs = s
            top = jnp.max(cs[0])
            for c in cs[1:]:
                top = jnp.maximum(top, jnp.max(c))
            return (kb >= 0) & (top > ATTN_DEAD_LOG)

        def body_of(kr, vr, tk, u):
            def body(s):
                kb, accs, cs = s
                accs, cs = sweep(kr, vr, tk, u, pl.multiple_of(kb * tk, tk), accs, cs, None)
                return kb - 1, accs, cs
            return body

        accs = [jnp.zeros((tq, dh), F32) for _ in heads]
        cs = [jnp.zeros((tq, 1), F32) for _ in heads]
        accs, cs = sweep(k_ref, v_ref, tq, u_new, r0, accs, cs, diag_mask)
        _, accs, cs = lax.while_loop(cond, body_of(k_ref, v_ref, tq, u_new), (qi - 1, accs, cs))
        if has_cache:
            _, accs, cs = lax.while_loop(cond, body_of(ck_ref, cv_ref, tk_cache, u_cache),
                                         (past // tk_cache - 1, accs, cs))
        for hs, acc in zip(heads, accs):
            o_ref[pl.ds(r0, tq), hs] = acc.astype(o_ref.dtype)
        return carry

    lax.fori_loop(0, l // tq, q_block, 0)


def _attention(q, k, v, cache_k, cache_v, *, bn, n_heads):
    m, d = q.shape
    l = m // bn
    dh = d // n_heads
    has_cache = cache_k is not None
    past = cache_k.shape[0] // bn if has_cache else 0
    tq = _tile(l, 256)
    tk_cache = _tile(past, 256) if has_cache else 0
    hp = ATTN_HEADS_PER_STEP if n_heads % ATTN_HEADS_PER_STEP == 0 else 1
    blk = lambda b, h: (b, h)
    in_specs = [pl.BlockSpec((l, hp * dh), blk)] * 3
    args = [q, k, v]
    if has_cache:
        in_specs += [pl.BlockSpec((past, hp * dh), blk)] * 2
        args += [cache_k, cache_v]
    return pl.pallas_call(
        functools.partial(_attn_kernel, tq=tq, tk_cache=tk_cache, dh=dh, scale=dh ** -0.5,
                          has_cache=has_cache),
        grid=(bn, n_heads // hp),
        in_specs=in_specs,
        out_specs=pl.BlockSpec((l, hp * dh), blk),
        out_shape=jax.ShapeDtypeStruct((m, d), BF16),
        compiler_params=_params("arbitrary", "arbitrary"),
        name="stick_breaking_attention",
    )(*args)


def kernel(x_prompt, x_sample, state_pool, cache_k, cache_v, c_prompt, c_sample, ada_w, ada_b, ln_g, ln_b,
           even_w_in, gmlp_ln_g, gmlp_ln_b, gmlp_ws, gmlp_b, pool_w, pool_scale, even_w_out, odd_w_qkv,
           odd_w_o, router_w, router_b, moe_w1, moe_b1, moe_w2, moe_b2):
    depth = ada_w.shape[0]
    d = x_prompt.shape[-1]
    n_heads = cache_k.shape[-2]
    d_a = gmlp_ln_g.shape[1] * gmlp_ln_g.shape[2]
    d_b = pool_scale.shape[-1]
    alpha = (2 * depth) ** 0.25
    nb_p = x_prompt.shape[0]
    trunks = [dict(x=x_prompt, b0=0, pos0=0, pool=None, ck=None, cv=None),
              dict(x=x_sample, b0=nb_p, pos0=cache_k.shape[2], pool=state_pool, ck=cache_k, cv=cache_v)]
    tp, ts = trunks
    for tr in trunks:
        tr["bn"], tr["l"] = tr["x"].shape[0], tr["x"].shape[1]
        tr["pools"], tr["gvs"], tr["ks"], tr["vs"] = [], [], [], []

    mods = _ada(jnp.concatenate([c_prompt, c_sample], axis=0), ada_w, ada_b)
    mods = mods.reshape(depth, 2, mods.shape[1], 3, d)

    def mod_of(tr, layer, sub):
        return mods[layer, sub, tr["b0"]:tr["b0"] + tr["bn"]]

    for tr in trunks:
        tr["h"] = _modulate(tr["x"], mod_of(tr, 0, 0), BF16).reshape(-1, d)
        tr["x"] = tr["x"].reshape(-1, d)

    for layer in range(depth):
        j = layer // 2
        for tr in trunks:
            bn, l = tr["bn"], tr["l"]
            h2 = tr["h"]
            if layer % 2 == 0:
                w_in = even_w_in[j]
                u = _mm(h2, w_in, 0, d_a, act="gelu", name="even_in_u").reshape(bn, l, d_a)
                v = _mm(h2, w_in, d_a, d_a, act="gelu", name="even_in_v").reshape(bn, l, d_a)
                p = _mm(h2, w_in, 2 * d_a, d_b, name="even_in_p").reshape(bn, l, d_b)
                n_hist = POOL_HALO - 1
                hist = jnp.zeros((bn, n_hist, d_b), F32) if tr["pool"] is None else tr["pool"][j]
                ab, gv = _even_mix(u, v, p, hist, gmlp_ln_g[j], gmlp_ln_b[j], gmlp_ws[j], gmlp_b[j],
                                   pool_w[j], pool_scale[j], pos0=tr["pos0"], emit_v=tr["pool"] is not None)
                tr["pools"].append(p[:, l - n_hist:] if l >= n_hist
                                   else jnp.concatenate([hist, p], axis=1)[:, -n_hist:])
                tr["gvs"].append(gv)
                y = _mm(ab.reshape(bn * l, d_a + d_b), even_w_out[j], 0, d, name="even_out")
            else:
                w_qkv = odd_w_qkv[j]
                q = _mm(h2, w_qkv, 0, d, out_dtype=BF16, name="attn_q")
                k = _mm(h2, w_qkv, d, d, name="attn_k")
                v = _mm(h2, w_qkv, 2 * d, d, name="attn_v")
                ck = None if tr["ck"] is None else tr["ck"][j].reshape(-1, d)
                cv = None if tr["cv"] is None else tr["cv"][j].reshape(-1, d)
                o = _attention(q, k, v, ck, cv, bn=bn, n_heads=n_heads)
                tr["ks"].append(k.reshape(bn, l, n_heads, d // n_heads))
                tr["vs"].append(v.reshape(bn, l, n_heads, d // n_heads))
                y = _mm(o, odd_w_o[j], 0, d, name="attn_out")
            tr["y"] = y
        tp["x"], ts["x"], h_all = _lnres(
            tp["x"], tp["y"], mod_of(tp, layer, 0), ts["x"], ts["y"], mod_of(ts, layer, 0),
            ln_g[layer, 0], ln_b[layer, 0], mod_of(tp, layer, 1), mod_of(ts, layer, 1),
            l_p=tp["l"], alpha=alpha)
        y_rows, pos, gates = _moe_rows(h_all, layer, router_w[layer], router_b[layer], moe_w1, moe_b1[layer],
                                       moe_w2, moe_b2[layer])
        last = layer + 1 == depth
        tp["x"], ts["x"], tp["h"], ts["h"] = _combine(
            tp["x"], ts["x"], y_rows, pos, gates, mod_of(tp, layer, 1), mod_of(ts, layer, 1),
            ln_g[layer, 1], ln_b[layer, 1], None if last else mod_of(tp, layer + 1, 0),
            None if last else mod_of(ts, layer + 1, 0), l_p=tp["l"], alpha=alpha, h_dtype=BF16)

    for tr in trunks:
        tr["x"] = tr["x"].reshape(tr["bn"], tr["l"], d)
    return (tp["x"], ts["x"], jnp.stack(tp["pools"]), jnp.stack(ts["pools"]), jnp.stack(ts["gvs"]),
            jnp.stack(tp["ks"]), jnp.stack(tp["vs"]), jnp.stack(ts["ks"]), jnp.stack(ts["vs"]))
```

```python
import functools

import jax
import jax.numpy as jnp
from jax import lax
from jax.experimental import pallas as pl
from jax.experimental.pallas import tpu as pltpu

F32 = jnp.float32
BF16 = jnp.bfloat16

POOL_WINDOWS = (2, 4, 8, 16)
POOL_HALO = 16
TOP_K = 4
GMLP_CHUNK = 128
SWIGLU_ALPHA = 1.702
SWIGLU_LIMIT = 7.0
LN_EPS = 1e-5

V7X_VMEM_LIMIT_BYTES = 56 * 1024 * 1024
V7X_MXU_DIM = 256

MOE_ROWS = 512
FFN1_COLS = 1024
FFN2_COLS = 2048
ATTN_HEADS_PER_STEP = 2
ATTN_DEAD_LOG = -110.0


def _tile(n, pref, mult=8):
    if n <= pref:
        return n
    t = (pref // mult) * mult
    while t >= mult:
        if n % t == 0:
            return t
        t -= mult
    return n


def _params(*sem):
    return pltpu.CompilerParams(dimension_semantics=sem, vmem_limit_bytes=V7X_VMEM_LIMIT_BYTES)


def _layer_norm_rows(t, g, b):
    mu = jnp.mean(t, axis=-1, keepdims=True)
    tc = t - mu
    var = jnp.mean(tc * tc, axis=-1, keepdims=True)
    return tc * lax.rsqrt(var + LN_EPS) * g + b


def _ada_kernel(c_ref, w_ref, b_ref, o_ref):
    c = c_ref[...]
    s = (c * jax.nn.sigmoid(c)).astype(BF16)
    o_ref[...] = jnp.dot(s, w_ref[...].astype(BF16), preferred_element_type=F32) + b_ref[...]


def _ada(c_all, ada_w, ada_b):
    nb, d = c_all.shape
    n = ada_w.shape[-1]
    w = ada_w.reshape(-1, d, n)
    b = ada_b.reshape(-1, 1, n)
    npair = w.shape[0]
    tn = _tile(n, 512, 128)
    return pl.pallas_call(
        _ada_kernel,
        grid=(npair, n // tn),
        in_specs=[pl.BlockSpec((nb, d), lambda p, j: (0, 0)),
                  pl.BlockSpec((None, d, tn), lambda p, j: (p, 0, j)),
                  pl.BlockSpec((None, 1, tn), lambda p, j: (p, 0, j))],
        out_specs=pl.BlockSpec((None, nb, tn), lambda p, j: (p, 0, j)),
        out_shape=jax.ShapeDtypeStruct((npair, nb, n), F32),
        compiler_params=_params("arbitrary", "arbitrary"),
        name="ada_modulation",
    )(c_all, w, b)


def _mod_kernel(x_ref, m_ref, h_ref):
    h_ref[...] = (x_ref[...] * (1.0 + m_ref[1:2, :]) + m_ref[0:1, :]).astype(h_ref.dtype)


def _modulate(x, mod, out_dtype):
    bn, l, d = x.shape
    tl = _tile(l, 256)
    return pl.pallas_call(
        _mod_kernel,
        grid=(bn, l // tl),
        in_specs=[pl.BlockSpec((None, tl, d), lambda b, i: (b, i, 0)),
                  pl.BlockSpec((None, 3, d), lambda b, i: (b, 0, 0))],
        out_specs=pl.BlockSpec((None, tl, d), lambda b, i: (b, i, 0)),
        out_shape=jax.ShapeDtypeStruct((bn, l, d), out_dtype),
        compiler_params=_params("arbitrary", "arbitrary"),
        name="modulate",
    )(x, mod)


def _mm_kernel(a_ref, w_ref, o_ref, wb_ref, *, act):
    @pl.when(pl.program_id(1) == 0)
    def _():
        wb_ref[...] = w_ref[...].astype(BF16)

    acc = jnp.dot(a_ref[...], wb_ref[...], preferred_element_type=F32)
    if act == "gelu":
        acc = jax.nn.gelu(acc, approximate=True)
    o_ref[...] = acc.astype(o_ref.dtype)


def _mm(a, w, col0, n, *, act=None, out_dtype=F32, name="matmul"):
    m, k = a.shape
    tm = _tile(m, 1024)
    tn = _tile(n, 512, 128)
    assert col0 % tn == 0
    joff = col0 // tn
    return pl.pallas_call(
        functools.partial(_mm_kernel, act=act),
        grid=(n // tn, m // tm),
        in_specs=[pl.BlockSpec((tm, k), lambda j, i: (i, 0)),
                  pl.BlockSpec((k, tn), lambda j, i: (0, j + joff))],
        out_specs=pl.BlockSpec((tm, tn), lambda j, i: (i, j)),
        out_shape=jax.ShapeDtypeStruct((m, n), out_dtype),
        scratch_shapes=[pltpu.VMEM((k, tn), BF16)],
        compiler_params=_params("arbitrary", "arbitrary"),
        name=name,
    )(a, w)


def _even_kernel(u_ref, v_ref, p_ref, pprev_ref, hist_ref, glng_ref, glnb_ref, ws_ref, gbt_ref,
                 pw_ref, ps_ref, *rest, tl, chunk, n_groups, pos0, emit_v):
    if emit_v:
        ab_ref, gv_ref, pwb_ref = rest
    else:
        ab_ref, pwb_ref = rest
        gv_ref = None
    b = pl.program_id(0)
    i = pl.program_id(1)
    d_a = u_ref.shape[-1]
    d_b = p_ref.shape[-1]
    ag = d_a // n_groups
    bg = d_b // len(POOL_WINDOWS)

    @pl.when((b == 0) & (i == 0))
    def _():
        pwb_ref[...] = pw_ref[...].astype(BF16)

    tri = (lax.broadcasted_iota(jnp.int32, (chunk, chunk), 0)
           >= lax.broadcasted_iota(jnp.int32, (chunk, chunk), 1))
    for g in range(n_groups):
        sl = slice(g * ag, (g + 1) * ag)
        vn = _layer_norm_rows(v_ref[:, sl], glng_ref[:, sl], glnb_ref[:, sl])
        if emit_v:
            gv_ref[:, sl] = vn
        wm = jnp.where(tri, ws_ref[g][:chunk, :chunk], 0.0).astype(BF16)
        vb = vn.astype(BF16)
        bias = gbt_ref[:chunk, g:g + 1]
        for c in range(tl // chunk):
            rows = slice(c * chunk, (c + 1) * chunk)
            mixed = jnp.dot(wm, vb[rows], preferred_element_type=F32) + bias
            ab_ref[rows, sl] = (u_ref[rows, sl] * mixed).astype(ab_ref.dtype)

    prev = jnp.where(i == 0, hist_ref[...], pprev_ref[...])
    p = p_ref[...]
    ext = jnp.concatenate([prev, p], axis=0)
    pos = pos0 + i * tl + lax.broadcasted_iota(jnp.int32, (tl, 1), 0)
    for gi, w in enumerate(POOL_WINDOWS):
        sl = slice(gi * bg, (gi + 1) * bg)
        s = ext[:, sl]
        step = 1
        while step < w:
            s = s + pltpu.roll(s, step, 0)
            step *= 2
        cnt = jnp.minimum(w, pos + 1).astype(F32)
        pooled = s[POOL_HALO:, :] / cnt - p[:, sl]
        mixed = jnp.dot(pooled.astype(BF16), pwb_ref[gi], preferred_element_type=F32) * ps_ref[:, sl]
        ab_ref[:, d_a + gi * bg:d_a + (gi + 1) * bg] = mixed.astype(ab_ref.dtype)


def _even_mix(u, v, p, hist, gln_g, gln_b, gws, gb, pool_w, pool_scale, *, pos0, emit_v):
    bn, l, d_a = u.shape
    d_b = p.shape[-1]
    n_groups = gws.shape[0]
    chunk = min(l, GMLP_CHUNK)
    tl = _tile(l, 2 * GMLP_CHUNK, chunk)
    assert l % chunk == 0 and tl % chunk == 0 and tl % POOL_HALO == 0
    halo_blocks = tl // POOL_HALO
    hist16 = jnp.concatenate(
        [jnp.zeros((bn, POOL_HALO - hist.shape[1], d_b), F32), hist], axis=1)
    out_shape = [jax.ShapeDtypeStruct((bn, l, d_a + d_b), BF16)]
    out_specs = [pl.BlockSpec((None, tl, d_a + d_b), lambda b, i: (b, i, 0))]
    if emit_v:
        out_shape.append(jax.ShapeDtypeStruct((bn, l, d_a), F32))
        out_specs.append(pl.BlockSpec((None, tl, d_a), lambda b, i: (b, i, 0)))
    row_blk = lambda b, i: (b, i, 0)
    whole2 = lambda b, i: (0, 0)
    whole3 = lambda b, i: (0, 0, 0)
    res = pl.pallas_call(
        functools.partial(_even_kernel, tl=tl, chunk=chunk, n_groups=n_groups, pos0=pos0, emit_v=emit_v),
        grid=(bn, l // tl),
        in_specs=[pl.BlockSpec((None, tl, d_a), row_blk),
                  pl.BlockSpec((None, tl, d_a), row_blk),
                  pl.BlockSpec((None, tl, d_b), row_blk),
                  pl.BlockSpec((None, POOL_HALO, d_b),
                               lambda b, i: (b, jnp.maximum(i * halo_blocks - 1, 0), 0)),
                  pl.BlockSpec((None, POOL_HALO, d_b), lambda b, i: (b, 0, 0)),
                  pl.BlockSpec((1, d_a), whole2),
                  pl.BlockSpec((1, d_a), whole2),
                  pl.BlockSpec(gws.shape, whole3),
                  pl.BlockSpec((gb.shape[1], gb.shape[0]), whole2),
                  pl.BlockSpec(pool_w.shape, whole3),
                  pl.BlockSpec((1, d_b), whole2)],
        out_specs=out_specs,
        out_shape=out_shape,
        scratch_shapes=[pltpu.VMEM(pool_w.shape, BF16)],
        compiler_params=_params("arbitrary", "arbitrary"),
        name="even_mixer",
    )(u, v, p, p, hist16, gln_g.reshape(1, d_a), gln_b.reshape(1, d_a), gws, gb.T, pool_w,
      pool_scale.reshape(1, d_b))
    return res if emit_v else (res[0], None)


def _post_norm(x, y, gate, g, b, alpha):
    return _layer_norm_rows(alpha * x + (1.0 + gate) * y, g, b)


def _pack_bf16_pairs(h):
    half = h.shape[-1] // 2
    bits = lax.bitcast_convert_type(h.astype(BF16).astype(F32), jnp.uint32)
    return (bits[:, :half] >> 16) | (bits[:, half:] & jnp.uint32(0xFFFF0000))


LANES = 128


def _token_major_store(ref, val, tok0):
    n, width = val.shape
    w = width // LANES
    for c in range(w):
        ref[pl.ds(tok0 * w + c, n, stride=w), :] = val[:, c * LANES:(c + 1) * LANES]


def _token_major_chunk(ref, c, n, w):
    return ref[pl.ds(c, n, stride=w), :]


def _unpack_token_major(ref, n):
    w = ref.shape[0] // n
    chunks = [_token_major_chunk(ref, c, n, w) for c in range(w)]
    lo = [lax.bitcast_convert_type(p << 16, F32).astype(BF16) for p in chunks]
    hi = [lax.bitcast_convert_type(p & jnp.uint32(0xFFFF0000), F32).astype(BF16) for p in chunks]
    return jnp.concatenate(lo + hi, axis=-1)


def _norm_mod_store(x, y, m, g, b, mn, xo_ref, h_ref, row0, alpha):
    xn = _post_norm(x, y, m[2:3, :], g, b, alpha)
    rows = slice(row0, row0 + x.shape[0])
    xo_ref[rows, :] = xn
    if h_ref is not None:
        h = xn * (1.0 + mn[1:2, :]) + mn[0:1, :]
        if h_ref.dtype == jnp.uint32:
            _token_major_store(h_ref, _pack_bf16_pairs(h), row0)
        else:
            h_ref[rows, :] = h.astype(h_ref.dtype)


def _tile_plan(t_p, l_p, bs, ls):
    tl = bs * ls
    assert l_p % tl == 0 and tl % 8 == 0
    n_p = t_p // tl
    per_stream = l_p // tl
    p_blk = lambda s: (jnp.minimum(s, n_p - 1), 0)
    p_mod = lambda s: (jnp.minimum(s, n_p - 1) // per_stream, 0, 0)
    return tl, n_p, p_blk, p_mod


def _lnres_kernel(xp_ref, yp_ref, mp_ref, xs_ref, ys_ref, ms_ref, g_ref, b_ref, mpn_ref, msn_ref,
                  xpo_ref, xso_ref, h_ref, *, n_p, bs, ls, alpha):
    s = pl.program_id(0)

    @pl.when(s < n_p)
    def _():
        _norm_mod_store(xp_ref[...], yp_ref[...], mp_ref[...], g_ref[...], b_ref[...], mpn_ref[...],
                        xpo_ref, h_ref, 0, alpha)

    @pl.when(s == n_p)
    def _():
        for q in range(bs):
            rows = slice(q * ls, (q + 1) * ls)
            _norm_mod_store(xs_ref[rows, :], ys_ref[rows, :], ms_ref[q], g_ref[...], b_ref[...], msn_ref[q],
                            xso_ref, h_ref, q * ls, alpha)


def _lnres(xp, yp, mp, xs, ys, ms, g, b, mpn, msn, *, l_p, alpha):
    t_p, d = xp.shape
    bs = ms.shape[0]
    ls = xs.shape[0] // bs
    tl, n_p, p_blk, p_mod = _tile_plan(t_p, l_p, bs, ls)
    whole2 = lambda s: (0, 0)
    whole3 = lambda s: (0, 0, 0)
    w = d // 2 // LANES
    return pl.pallas_call(
        functools.partial(_lnres_kernel, n_p=n_p, bs=bs, ls=ls, alpha=alpha),
        grid=(n_p + 1,),
        in_specs=[pl.BlockSpec((tl, d), p_blk), pl.BlockSpec((tl, d), p_blk),
                  pl.BlockSpec((None, 3, d), p_mod),
                  pl.BlockSpec((tl, d), whole2), pl.BlockSpec((tl, d), whole2),
                  pl.BlockSpec((bs, 3, d), whole3),
                  pl.BlockSpec((1, d), whole2), pl.BlockSpec((1, d), whole2),
                  pl.BlockSpec((None, 3, d), p_mod), pl.BlockSpec((bs, 3, d), whole3)],
        out_specs=[pl.BlockSpec((tl, d), p_blk), pl.BlockSpec((tl, d), whole2),
                   pl.BlockSpec((tl * w, LANES), lambda s: (s, 0))],
        out_shape=[jax.ShapeDtypeStruct((t_p, d), F32), jax.ShapeDtypeStruct((tl, d), F32),
                   jax.ShapeDtypeStruct(((t_p + tl) * w, LANES), jnp.uint32)],
        compiler_params=_params("arbitrary"),
        name="post_norm_modulate",
    )(xp, yp, mp, xs, ys, ms, g.reshape(1, d), b.reshape(1, d), mpn, msn)


def _router_kernel(h_ref, w_ref, b_ref, idx_ref, gate_ref):
    logits = jnp.dot(_unpack_token_major(h_ref, idx_ref.shape[0]), w_ref[...].astype(BF16),
                     preferred_element_type=F32) + b_ref[...]
    n_exp = logits.shape[-1]
    lane = lax.broadcasted_iota(jnp.int32, logits.shape, 1)
    vals = []
    for k in range(TOP_K):
        m = jnp.max(logits, axis=-1, keepdims=True)
        sel = jnp.min(jnp.where(logits == m, lane, n_exp), axis=-1, keepdims=True)
        idx_ref[:, k:k + 1] = sel
        vals.append(m)
        logits = jnp.where(lane == sel, -jnp.inf, logits)
    exps = [jnp.exp(v - vals[0]) for v in vals]
    denom = exps[0]
    for e in exps[1:]:
        denom = denom + e
    for k in range(TOP_K):
        gate_ref[:, k:k + 1] = exps[k] / denom


def _router(h, w_r, b_r):
    d, e = w_r.shape
    w = d // 2 // LANES
    t = h.shape[0] // w
    tt = _tile(t, 512)
    return pl.pallas_call(
        _router_kernel,
        grid=(t // tt,),
        in_specs=[pl.BlockSpec((tt * w, LANES), lambda i: (i, 0)),
                  pl.BlockSpec((d, e), lambda i: (0, 0)),
                  pl.BlockSpec((1, e), lambda i: (0, 0))],
        out_specs=[pl.BlockSpec((tt, TOP_K), lambda i: (i, 0)),
                   pl.BlockSpec((tt, TOP_K), lambda i: (i, 0))],
        out_shape=[jax.ShapeDtypeStruct((t, TOP_K), jnp.int32),
                   jax.ShapeDtypeStruct((t, TOP_K), F32)],
        compiler_params=_params("arbitrary"),
        name="router_topk",
    )(h, w_r, b_r.reshape(1, e))


GATHER_UNROLL = 8


def _token_gather(idx_smem, src_token, dst, sem, n, w, base=0):
    def body(r, carry):
        pltpu.make_async_copy(src_token(idx_smem[base + r]), dst.at[pl.ds(pl.multiple_of(r * w, w), w)], sem).start()
        return carry
    lax.fori_loop(0, n, body, 0, unroll=GATHER_UNROLL)


def _token_gather_wait(dst, sem):
    pltpu.make_async_copy(dst, dst, sem).wait()


def _dispatch_kernel(nu_ref, tok_hbm, h_hbm, o_ref, tok_smem, buf, sem_i, sem_g, *, tm):
    i = pl.program_id(0)
    w = buf.shape[1] // tm

    def start_block(blk, slot):
        cp = pltpu.make_async_copy(tok_hbm.at[blk], tok_smem, sem_i)
        cp.start()
        cp.wait()
        _token_gather(tok_smem, lambda t: h_hbm.at[pl.ds(pl.multiple_of(t * w, w), w)], buf.at[slot],
                      sem_g.at[slot], tm, w)

    @pl.when(i == 0)
    def _():
        start_block(0, 0)

    @pl.when(i + 1 < nu_ref[0])
    def _():
        start_block(i + 1, (i + 1) % 2)

    @pl.when(i < nu_ref[0])
    def _():
        slot = i % 2
        _token_gather_wait(buf.at[slot], sem_g.at[slot])
        o_ref[...] = _unpack_token_major(buf.at[slot], tm)

    @pl.when(i >= nu_ref[0])
    def _():
        o_ref[...] = jnp.zeros_like(o_ref)


def _dispatch(h, row_tok, n_used, tm, w):
    n_blocks = row_tok.shape[0]
    d = 2 * w * LANES
    return pl.pallas_call(
        functools.partial(_dispatch_kernel, tm=tm),
        grid_spec=pltpu.PrefetchScalarGridSpec(
            num_scalar_prefetch=1,
            grid=(n_blocks,),
            in_specs=[pl.BlockSpec(memory_space=pl.ANY), pl.BlockSpec(memory_space=pl.ANY)],
            out_specs=pl.BlockSpec((tm, d), lambda i, nu: (i, 0)),
            scratch_shapes=[pltpu.SMEM((tm,), jnp.int32), pltpu.VMEM((2, tm * w, LANES), jnp.uint32),
                            pltpu.SemaphoreType.DMA, pltpu.SemaphoreType.DMA((2,))]),
        out_shape=jax.ShapeDtypeStruct((n_blocks * tm, d), BF16),
        compiler_params=_params("arbitrary"),
        name="moe_dispatch",
    )(n_used, row_tok, h)


def _deinterleave_matrix():
    g = V7X_MXU_DIM
    r = lax.broadcasted_iota(jnp.int32, (g, g), 0)
    c = lax.broadcasted_iota(jnp.int32, (g, g), 1)
    src = jnp.where(c < g // 2, 2 * c, 2 * (c - g // 2) + 1)
    return (r == src).astype(BF16)


def _stream_expert_weights(be_ref, first_ref, nxt_ref, nu_ref, w_hbm, stage, sem, convert, *, layer, tn):
    j = pl.program_id(0)
    i = pl.program_id(1)

    def tile_copy(e, jj):
        c0 = pl.multiple_of(jj * tn, tn)
        return pltpu.make_async_copy(w_hbm.at[layer, e, :, pl.ds(c0, tn)], stage, sem)

    @pl.when((j == 0) & (i == 0))
    def _():
        tile_copy(be_ref[0], 0).start()

    @pl.when((i < nu_ref[0]) & (first_ref[i] == 1))
    def _():
        tile_copy(be_ref[i], j).wait()
        convert()
        nxt = nxt_ref[i]

        @pl.when(nxt >= 0)
        def _():
            tile_copy(nxt, j).start()

        @pl.when((nxt < 0) & (j + 1 < pl.num_programs(0)))
        def _():
            tile_copy(be_ref[0], j + 1).start()


def _ffn1_kernel(be_ref, first_ref, nxt_ref, nu_ref, x_ref, w_hbm, b_ref, o_ref, stage, wb_ref, sem,
                 *, layer, tn):
    i = pl.program_id(1)
    valid = i < nu_ref[0]
    grp = V7X_MXU_DIM
    half = grp // 2

    def convert():
        perm = _deinterleave_matrix()
        for g in range(tn // grp):
            cols = slice(g * grp, (g + 1) * grp)
            wg = stage[:, cols].astype(BF16)
            wb_ref[:, cols] = jnp.dot(wg, perm, preferred_element_type=F32).astype(BF16)

    _stream_expert_weights(be_ref, first_ref, nxt_ref, nu_ref, w_hbm, stage, sem, convert, layer=layer, tn=tn)

    @pl.when(valid)
    def _():
        hg = jnp.dot(x_ref[...], wb_ref[...], preferred_element_type=F32) + b_ref[...]
        for g in range(tn // grp):
            glu = jnp.minimum(hg[:, g * grp:g * grp + half], SWIGLU_LIMIT)
            lin = jnp.clip(hg[:, g * grp + half:(g + 1) * grp], -SWIGLU_LIMIT, SWIGLU_LIMIT)
            act = glu * jax.nn.sigmoid(SWIGLU_ALPHA * glu) * (lin + 1.0)
            o_ref[:, g * half:(g + 1) * half] = act.astype(o_ref.dtype)

    @pl.when(jnp.logical_not(valid))
    def _():
        o_ref[...] = jnp.zeros_like(o_ref)


def _ffn2_kernel(be_ref, first_ref, nxt_ref, nu_ref, a_ref, w_hbm, b_ref, o_ref, stage, wb_ref, sem,
                 *, layer, tn):
    i = pl.program_id(1)
    valid = i < nu_ref[0]

    def convert():
        wb_ref[...] = stage[...].astype(BF16)

    _stream_expert_weights(be_ref, first_ref, nxt_ref, nu_ref, w_hbm, stage, sem, convert, layer=layer, tn=tn)

    @pl.when(valid)
    def _():
        y = jnp.dot(a_ref[...], wb_ref[...], preferred_element_type=F32) + b_ref[...]
        for c in range(tn // LANES):
            o_ref[:, c, :] = y[:, c * LANES:(c + 1) * LANES]

    @pl.when(jnp.logical_not(valid))
    def _():
        o_ref[...] = jnp.zeros_like(o_ref)


def _grouped_mm(kernel, x, w, b, plan, *, layer, tm, tn, n_out, out_dtype, name, chunked_out=False):
    n_rows, k = x.shape
    n_blocks = n_rows // tm
    n = w.shape[-1]
    last = lambda nu: jnp.maximum(nu[0] - 1, 0)
    x_map = lambda j, i, be, fi, nx, nu: (jnp.minimum(i, last(nu)), 0)
    b_map = lambda j, i, be, fi, nx, nu: (be[jnp.minimum(i, last(nu))], 0, j)
    tn_out = n_out // (n // tn)
    if chunked_out:
        out_spec = pl.BlockSpec((tm, tn_out // LANES, LANES), lambda j, i, be, fi, nx, nu: (i, j, 0))
        out_shape = jax.ShapeDtypeStruct((n_rows, n_out // LANES, LANES), out_dtype)
    else:
        out_spec = pl.BlockSpec((tm, tn_out), lambda j, i, be, fi, nx, nu: (i, j))
        out_shape = jax.ShapeDtypeStruct((n_rows, n_out), out_dtype)
    return pl.pallas_call(
        functools.partial(kernel, layer=layer, tn=tn),
        grid_spec=pltpu.PrefetchScalarGridSpec(
            num_scalar_prefetch=4,
            grid=(n // tn, n_blocks),
            in_specs=[pl.BlockSpec((tm, k), x_map),
                      pl.BlockSpec(memory_space=pl.ANY),
                      pl.BlockSpec((None, 1, tn), b_map)],
            out_specs=out_spec,
            scratch_shapes=[pltpu.VMEM((k, tn), F32), pltpu.VMEM((k, tn), BF16), pltpu.SemaphoreType.DMA]),
        out_shape=out_shape,
        compiler_params=_params("arbitrary", "arbitrary"),
        name=name,
    )(plan["block_e"], plan["is_first"], plan["next_e"], plan["n_used"], x, w, b)


def _combine_kernel(pos_hbm, y_hbm, xp_ref, xs_ref, gates_ref, mp_ref, ms_ref, g_ref, b_ref, *rest,
                    tl, n_p, bs, ls, alpha, has_next):
    if has_next:
        mpn_ref, msn_ref, xpo_ref, xso_ref, hp_ref, hs_ref, pos_smem, buf, y_buf, sem_i, sem_g = rest
    else:
        xpo_ref, xso_ref, pos_smem, buf, y_buf, sem_i, sem_g = rest
        mpn_ref = msn_ref = hp_ref = hs_ref = None
    s = pl.program_id(0)
    w = y_hbm.shape[1]

    def start_tile(t, slot):
        cp = pltpu.make_async_copy(pos_hbm.at[t], pos_smem, sem_i)
        cp.start()
        cp.wait()
        for k in range(TOP_K):
            _token_gather(pos_smem, lambda r: y_hbm.at[r], buf.at[slot, k], sem_g.at[slot], tl, w, base=k * tl)

    @pl.when(s == 0)
    def _():
        start_tile(0, 0)

    @pl.when(s < n_p)
    def _():
        start_tile(s + 1, (s + 1) % 2)

    slot = s % 2
    for k in range(TOP_K):
        _token_gather_wait(buf.at[slot, k], sem_g.at[slot])
    for c in range(w):
        y = gates_ref[:, 0:1] * _token_major_chunk(buf.at[slot, 0], c, tl, w)
        for k in range(1, TOP_K):
            y = y + gates_ref[:, k:k + 1] * _token_major_chunk(buf.at[slot, k], c, tl, w)
        y_buf[:, c * LANES:(c + 1) * LANES] = y

    @pl.when(s < n_p)
    def _():
        _norm_mod_store(xp_ref[...], y_buf[...], mp_ref[...], g_ref[...], b_ref[...],
                        mpn_ref[...] if has_next else None, xpo_ref, hp_ref, 0, alpha)

    @pl.when(s == n_p)
    def _():
        for q in range(bs):
            rows = slice(q * ls, (q + 1) * ls)
            _norm_mod_store(xs_ref[rows, :], y_buf[rows, :], ms_ref[q], g_ref[...], b_ref[...],
                            msn_ref[q] if has_next else None, xso_ref, hs_ref, q * ls, alpha)


def _combine(xp, xs, y_rows, pos, gates, mp, ms, g, b, mpn, msn, *, l_p, alpha, h_dtype):
    t_p, d = xp.shape
    bs = ms.shape[0]
    ls = xs.shape[0] // bs
    tl, n_p, p_blk, p_mod = _tile_plan(t_p, l_p, bs, ls)
    has_next = mpn is not None
    pos_tiles = pos.reshape(n_p + 1, tl, TOP_K).transpose(0, 2, 1).reshape(n_p + 1, TOP_K * tl)
    whole2 = lambda s: (0, 0)
    whole3 = lambda s: (0, 0, 0)
    in_specs = [pl.BlockSpec(memory_space=pl.ANY), pl.BlockSpec(memory_space=pl.ANY),
                pl.BlockSpec((tl, d), p_blk), pl.BlockSpec((tl, d), whole2),
                pl.BlockSpec((tl, TOP_K), lambda s: (s, 0)),
                pl.BlockSpec((None, 3, d), p_mod), pl.BlockSpec((bs, 3, d), whole3),
                pl.BlockSpec((1, d), whole2), pl.BlockSpec((1, d), whole2)]
    args = [pos_tiles, y_rows, xp, xs, gates, mp, ms, g.reshape(1, d), b.reshape(1, d)]
    out_specs = [pl.BlockSpec((tl, d), p_blk), pl.BlockSpec((tl, d), whole2)]
    out_shape = [jax.ShapeDtypeStruct((t_p, d), F32), jax.ShapeDtypeStruct((tl, d), F32)]
    if has_next:
        in_specs += [pl.BlockSpec((None, 3, d), p_mod), pl.BlockSpec((bs, 3, d), whole3)]
        args += [mpn, msn]
        out_specs += [pl.BlockSpec((tl, d), p_blk), pl.BlockSpec((tl, d), whole2)]
        out_shape += [jax.ShapeDtypeStruct((t_p, d), h_dtype), jax.ShapeDtypeStruct((tl, d), h_dtype)]
    res = pl.pallas_call(
        functools.partial(_combine_kernel, tl=tl, n_p=n_p, bs=bs, ls=ls, alpha=alpha, has_next=has_next),
        grid=(n_p + 1,),
        in_specs=in_specs,
        out_specs=out_specs,
        out_shape=out_shape,
        scratch_shapes=[pltpu.SMEM((TOP_K * tl,), jnp.int32), pltpu.VMEM((2, TOP_K, tl * (d // LANES), LANES), F32),
                        pltpu.VMEM((tl, d), F32), pltpu.SemaphoreType.DMA, pltpu.SemaphoreType.DMA((2,))],
        compiler_params=_params("arbitrary"),
        name="moe_combine_post_norm",
    )(*args)
    return res if has_next else (res[0], res[1], None, None)


def _moe_plan(idx, n_experts, tm):
    t = idx.shape[0]
    n_assign = t * TOP_K
    flat_e = idx.reshape(-1)
    order = jnp.argsort(flat_e).astype(jnp.int32)
    counts = jnp.sum((flat_e[:, None] == jnp.arange(n_experts, dtype=jnp.int32)[None, :]).astype(jnp.int32), axis=0)
    padded = (counts + tm - 1) // tm * tm
    start = jnp.cumsum(counts) - counts
    pad_end = jnp.cumsum(padded)
    pad_start = pad_end - padded
    n_blocks = -(-(n_assign + n_experts * (tm - 1)) // tm)
    n_used = (pad_end[-1] // tm).astype(jnp.int32).reshape(1)
    blk = jnp.arange(n_blocks, dtype=jnp.int32)
    block_e = jnp.minimum(jnp.sum((blk[:, None] * tm >= pad_end[None, :]).astype(jnp.int32), axis=1),
                          n_experts - 1).astype(jnp.int32)
    is_first = jnp.concatenate([jnp.ones((1,), jnp.int32),
                                (block_e[1:] != block_e[:-1]).astype(jnp.int32)])
    run_start = jnp.where((is_first == 1) & (blk < n_used[0]), blk, n_blocks)
    later_start = jnp.concatenate([lax.cummin(run_start[::-1])[::-1][1:], jnp.full((1,), n_blocks, jnp.int32)])
    next_e = jnp.where(later_start < n_blocks, block_e[jnp.minimum(later_start, n_blocks - 1)], -1).astype(jnp.int32)
    off = jnp.arange(tm, dtype=jnp.int32)[None, :] + (blk * tm - pad_start[block_e])[:, None]
    s = jnp.clip(start[block_e][:, None] + off, 0, n_assign - 1)
    row_tok = jnp.where(off < counts[block_e][:, None], order[s] // TOP_K, 0).astype(jnp.int32)
    e_sorted = flat_e[order]
    dest = (pad_start[e_sorted] + jnp.arange(n_assign, dtype=jnp.int32) - start[e_sorted]).astype(jnp.int32)
    _, pos = lax.sort_key_val(order, dest)
    plan = dict(block_e=block_e, is_first=is_first, next_e=next_e, n_used=n_used)
    return row_tok, pos.reshape(t, TOP_K), plan


def _moe_rows(h_all, layer, w_r, b_r, w1, b1, w2, b2):
    d, n_experts = w_r.shape
    d_e2 = w1.shape[-1]
    tm = MOE_ROWS
    idx, gates = _router(h_all, w_r, b_r)
    row_tok, pos, plan = _moe_plan(idx, n_experts, tm)
    xs = _dispatch(h_all, row_tok, plan["n_used"], tm, d // 2 // LANES)
    grp = V7X_MXU_DIM
    b1p = b1.reshape(n_experts, d_e2 // grp, grp // 2, 2).transpose(0, 1, 3, 2).reshape(n_experts, 1, d_e2)
    tn1 = _tile(d_e2, FFN1_COLS, grp)
    act = _grouped_mm(_ffn1_kernel, xs, w1, b1p, plan, layer=layer, tm=tm, tn=tn1, n_out=d_e2 // 2,
                      out_dtype=BF16, name="moe_ffn1")
    tn2 = _tile(d, FFN2_COLS, 128)
    y_rows = _grouped_mm(_ffn2_kernel, act, w2, b2.reshape(n_experts, 1, d), plan, layer=layer, tm=tm, tn=tn2,
                         n_out=d, out_dtype=F32, name="moe_ffn2", chunked_out=True)
    return y_rows, pos, gates


def _after_matrix(n):
    return (lax.broadcasted_iota(jnp.int32, (n, n), 0) > lax.broadcasted_iota(jnp.int32, (n, n), 1)).astype(BF16)


def _sb_block(q, kblk, vblk, acc, c, u, scale, mask):
    z = lax.dot_general(q, kblk.astype(BF16), (((1,), (1,)), ((), ())),
                        preferred_element_type=F32) * scale
    sp = jnp.maximum(z, 0.0) + jnp.log(1.0 + jnp.exp(-jnp.abs(z)))
    lk = -sp if mask is None else jnp.where(mask, -sp, 0.0)
    hi = lk.astype(BF16)
    lo = (lk - hi.astype(F32)).astype(BF16)
    after = jnp.dot(hi, u, preferred_element_type=F32) + jnp.dot(lo, u, preferred_element_type=F32) + c
    w = jnp.exp(z - sp + after)
    if mask is not None:
        w = jnp.where(mask, w, 0.0)
    acc = acc + jnp.dot(w.astype(BF16), vblk.astype(BF16), preferred_element_type=F32)
    c = c + jnp.sum(lk, axis=-1, keepdims=True)
    return acc, c


def _attn_kernel(*refs, tq, tk_cache, dh, scale, has_cache):
    if has_cache:
        q_ref, k_ref, v_ref, ck_ref, cv_ref, o_ref = refs
    else:
        q_ref, k_ref, v_ref, o_ref = refs
    l = q_ref.shape[0]
    heads = [slice(h * dh, (h + 1) * dh) for h in range(q_ref.shape[1] // dh)]
    past = ck_ref.shape[0] if has_cache else 0
    u_new = _after_matrix(tq)
    u_cache = _after_matrix(tk_cache) if has_cache else None
    diag_mask = (lax.broadcasted_iota(jnp.int32, (tq, tq), 1) < lax.broadcasted_iota(jnp.int32, (tq, tq), 0))

    def q_block(qi, carry):
        r0 = pl.multiple_of(qi * tq, tq)
        qs = [q_ref[pl.ds(r0, tq), hs] for hs in heads]

        def sweep(kr, vr, tk, u, k0, accs, cs, mask):
            out = [_sb_block(q, kr[pl.ds(k0, tk), hs], vr[pl.ds(k0, tk), hs], acc, c, u, scale, mask)
                   for q, hs, acc, c in zip(qs, heads, accs, cs)]
            return [o[0] for o in out], [o[1] for o in out]

        def cond(s):
            kb, _, cs = s
            top = jnp.max(cs[0])
            for c in cs[1:]:
                top = jnp.maximum(top, jnp.max(c))
            return (kb >= 0) & (top > ATTN_DEAD_LOG)

        def body_of(kr, vr, tk, u):
            def body(s):
                kb, accs, cs = s
                accs, cs = sweep(kr, vr, tk, u, pl.multiple_of(kb * tk, tk), accs, cs, None)
                return kb - 1, accs, cs
            return body

        accs = [jnp.zeros((tq, dh), F32) for _ in heads]
        cs = [jnp.zeros((tq, 1), F32) for _ in heads]
        accs, cs = sweep(k_ref, v_ref, tq, u_new, r0, accs, cs, diag_mask)
        _, accs, cs = lax.while_loop(cond, body_of(k_ref, v_ref, tq, u_new), (qi - 1, accs, cs))
        if has_cache:
            _, accs, cs = lax.while_loop(cond, body_of(ck_ref, cv_ref, tk_cache, u_cache),
                                         (past // tk_cache - 1, accs, cs))
        for hs, acc in zip(heads, accs):
            o_ref[pl.ds(r0, tq), hs] = acc.astype(o_ref.dtype)
        return carry

    lax.fori_loop(0, l // tq, q_block, 0)


def _attention(q, k, v, cache_k, cache_v, *, bn, n_heads):
    m, d = q.shape
    l = m // bn
    dh = d // n_heads
    has_cache = cache_k is not None
    past = cache_k.shape[0] // bn if has_cache else 0
    tq = _tile(l, 256)
    tk_cache = _tile(past, 256) if has_cache else 0
    hp = ATTN_HEADS_PER_STEP if n_heads % ATTN_HEADS_PER_STEP == 0 else 1
    blk = lambda b, h: (b, h)
    in_specs = [pl.BlockSpec((l, hp * dh), blk)] * 3
    args = [q, k, v]
    if has_cache:
        in_specs += [pl.BlockSpec((past, hp * dh), blk)] * 2
        args += [cache_k, cache_v]
    return pl.pallas_call(
        functools.partial(_attn_kernel, tq=tq, tk_cache=tk_cache, dh=dh, scale=dh ** -0.5,
                          has_cache=has_cache),
        grid=(bn, n_heads // hp),
        in_specs=in_specs,
        out_specs=pl.BlockSpec((l, hp * dh), blk),
        out_shape=jax.ShapeDtypeStruct((m, d), BF16),
        compiler_params=_params("arbitrary", "arbitrary"),
        name="stick_breaking_attention",
    )(*args)


def kernel(x_prompt, x_sample, state_pool, cache_k, cache_v, c_prompt, c_sample, ada_w, ada_b, ln_g, ln_b,
           even_w_in, gmlp_ln_g, gmlp_ln_b, gmlp_ws, gmlp_b, pool_w, pool_scale, even_w_out, odd_w_qkv,
           odd_w_o, router_w, router_b, moe_w1, moe_b1, moe_w2, moe_b2):
    depth = ada_w.shape[0]
    d = x_prompt.shape[-1]
    n_heads = cache_k.shape[-2]
    d_a = gmlp_ln_g.shape[1] * gmlp_ln_g.shape[2]
    d_b = pool_scale.shape[-1]
    alpha = (2 * depth) ** 0.25
    nb_p = x_prompt.shape[0]
    trunks = [dict(x=x_prompt, b0=0, pos0=0, pool=None, ck=None, cv=None),
              dict(x=x_sample, b0=nb_p, pos0=cache_k.shape[2], pool=state_pool, ck=cache_k, cv=cache_v)]
    tp, ts = trunks
    for tr in trunks:
        tr["bn"], tr["l"] = tr["x"].shape[0], tr["x"].shape[1]
        tr["pools"], tr["gvs"], tr["ks"], tr["vs"] = [], [], [], []

    mods = _ada(jnp.concatenate([c_prompt, c_sample], axis=0), ada_w, ada_b)
    mods = mods.reshape(depth, 2, mods.shape[1], 3, d)

    def mod_of(tr, layer, sub):
        return mods[layer, sub, tr["b0"]:tr["b0"] + tr["bn"]]

    for tr in trunks:
        tr["h"] = _modulate(tr["x"], mod_of(tr, 0, 0), BF16).reshape(-1, d)
        tr["x"] = tr["x"].reshape(-1, d)

    for layer in range(depth):
        j = layer // 2
        for tr in trunks:
            bn, l = tr["bn"], tr["l"]
            h2 = tr["h"]
            if layer % 2 == 0:
                w_in = even_w_in[j]
                u = _mm(h2, w_in, 0, d_a, act="gelu", name="even_in_u").reshape(bn, l, d_a)
                v = _mm(h2, w_in, d_a, d_a, act="gelu", name="even_in_v").reshape(bn, l, d_a)
                p = _mm(h2, w_in, 2 * d_a, d_b, name="even_in_p").reshape(bn, l, d_b)
                n_hist = POOL_HALO - 1
                hist = jnp.zeros((bn, n_hist, d_b), F32) if tr["pool"] is None else tr["pool"][j]
                ab, gv = _even_mix(u, v, p, hist, gmlp_ln_g[j], gmlp_ln_b[j], gmlp_ws[j], gmlp_b[j],
                                   pool_w[j], pool_scale[j], pos0=tr["pos0"], emit_v=tr["pool"] is not None)
                tr["pools"].append(p[:, l - n_hist:] if l >= n_hist
                                   else jnp.concatenate([hist, p], axis=1)[:, -n_hist:])
                tr["gvs"].append(gv)
                y = _mm(ab.reshape(bn * l, d_a + d_b), even_w_out[j], 0, d, name="even_out")
            else:
                w_qkv = odd_w_qkv[j]
                q = _mm(h2, w_qkv, 0, d, out_dtype=BF16, name="attn_q")
                k = _mm(h2, w_qkv, d, d, name="attn_k")
                v = _mm(h2, w_qkv, 2 * d, d, name="attn_v")
                ck = None if tr["ck"] is None else tr["ck"][j].reshape(-1, d)
                cv = None if tr["cv"] is None else tr["cv"][j].reshape(-1, d)
                o = _attention(q, k, v, ck, cv, bn=bn, n_heads=n_heads)
                tr["ks"].append(k.reshape(bn, l, n_heads, d // n_heads))
                tr["vs"].append(v.reshape(bn, l, n_heads, d // n_heads))
                y = _mm(o, odd_w_o[j], 0, d, name="attn_out")
            tr["y"] = y
        tp["x"], ts["x"], h_all = _lnres(
            tp["x"], tp["y"], mod_of(tp, layer, 0), ts["x"], ts["y"], mod_of(ts, layer, 0),
            ln_g[layer, 0], ln_b[layer, 0], mod_of(tp, layer, 1), mod_of(ts, layer, 1),
            l_p=tp["l"], alpha=alpha)
        y_rows, pos, gates = _moe_rows(h_all, layer, router_w[layer], router_b[layer], moe_w1, moe_b1[layer],
                                       moe_w2, moe_b2[layer])
        last = layer + 1 == depth
        tp["x"], ts["x"], tp["h"], ts["h"] = _combine(
            tp["x"], ts["x"], y_rows, pos, gates, mod_of(tp, layer, 1), mod_of(ts, layer, 1),
            ln_g[layer, 1], ln_b[layer, 1], None if last else mod_of(tp, layer + 1, 0),
            None if last else mod_of(ts, layer + 1, 0), l_p=tp["l"], alpha=alpha, h_dtype=BF16)

    for tr in trunks:
        tr["x"] = tr["x"].reshape(tr["bn"], tr["l"], d)
    return (tp["x"], ts["x"], jnp.stack(tp["pools"]), jnp.stack(ts["pools"]), jnp.stack(ts["gvs"]),
            jnp.stack(tp["ks"]), jnp.stack(tp["vs"]), jnp.stack(ts["ks"]), jnp.stack(ts["vs"]))
```

```python
import functools

import jax
import jax.numpy as jnp
from jax import lax
from jax.experimental import pallas as pl
from jax.experimental.pallas import tpu as pltpu

F32 = jnp.float32
BF16 = jnp.bfloat16

POOL_WINDOWS = (2, 4, 8, 16)
POOL_HALO = 16
TOP_K = 4
GMLP_CHUNK = 128
SWIGLU_ALPHA = 1.702
SWIGLU_LIMIT = 7.0
LN_EPS = 1e-5

V7X_VMEM_LIMIT_BYTES = 56 * 1024 * 1024
V7X_MXU_DIM = 256
V7X_DMA_PRIORITIES = 2

MOE_ROWS = 512
FFN1_COLS = 1024
FFN2_COLS = 2048
ATTN_HEADS_PER_STEP = 2
ATTN_DEAD_LOG = -110.0


def _tile(n, pref, mult=8):
    if n <= pref:
        return n
    t = (pref // mult) * mult
    while t >= mult:
        if n % t == 0:
            return t
        t -= mult
    return n


def _params(*sem):
    return pltpu.CompilerParams(dimension_semantics=sem, vmem_limit_bytes=V7X_VMEM_LIMIT_BYTES)


def _layer_norm_rows(t, g, b):
    mu = jnp.mean(t, axis=-1, keepdims=True)
    tc = t - mu
    var = jnp.mean(tc * tc, axis=-1, keepdims=True)
    return tc * lax.rsqrt(var + LN_EPS) * g + b


def _ada_kernel(c_ref, w_ref, b_ref, o_ref):
    c = c_ref[...]
    s = (c * jax.nn.sigmoid(c)).astype(BF16)
    o_ref[...] = jnp.dot(s, w_ref[...].astype(BF16), preferred_element_type=F32) + b_ref[...]


def _ada(c_all, ada_w, ada_b):
    nb, d = c_all.shape
    n = ada_w.shape[-1]
    w = ada_w.reshape(-1, d, n)
    b = ada_b.reshape(-1, 1, n)
    npair = w.shape[0]
    tn = _tile(n, 512, 128)
    return pl.pallas_call(
        _ada_kernel,
        grid=(npair, n // tn),
        in_specs=[pl.BlockSpec((nb, d), lambda p, j: (0, 0)),
                  pl.BlockSpec((None, d, tn), lambda p, j: (p, 0, j)),
                  pl.BlockSpec((None, 1, tn), lambda p, j: (p, 0, j))],
        out_specs=pl.BlockSpec((None, nb, tn), lambda p, j: (p, 0, j)),
        out_shape=jax.ShapeDtypeStruct((npair, nb, n), F32),
        compiler_params=_params("arbitrary", "arbitrary"),
        name="ada_modulation",
    )(c_all, w, b)


def _mod_kernel(x_ref, m_ref, h_ref):
    h_ref[...] = (x_ref[...] * (1.0 + m_ref[1:2, :]) + m_ref[0:1, :]).astype(h_ref.dtype)


def _modulate(x, mod, out_dtype):
    bn, l, d = x.shape
    tl = _tile(l, 256)
    return pl.pallas_call(
        _mod_kernel,
        grid=(bn, l // tl),
        in_specs=[pl.BlockSpec((None, tl, d), lambda b, i: (b, i, 0)),
                  pl.BlockSpec((None, 3, d), lambda b, i: (b, 0, 0))],
        out_specs=pl.BlockSpec((None, tl, d), lambda b, i: (b, i, 0)),
        out_shape=jax.ShapeDtypeStruct((bn, l, d), out_dtype),
        compiler_params=_params("arbitrary", "arbitrary"),
        name="modulate",
    )(x, mod)


def _mm_kernel(a_ref, w_ref, o_ref, wb_ref, *, act):
    @pl.when(pl.program_id(1) == 0)
    def _():
        wb_ref[...] = w_ref[...].astype(BF16)

    acc = jnp.dot(a_ref[...], wb_ref[...], preferred_element_type=F32)
    if act == "gelu":
        acc = jax.nn.gelu(acc, approximate=True)
    o_ref[...] = acc.astype(o_ref.dtype)


def _mm(a, w, col0, n, *, act=None, out_dtype=F32, name="matmul"):
    m, k = a.shape
    tm = _tile(m, 1024)
    tn = _tile(n, 512, 128)
    assert col0 % tn == 0
    joff = col0 // tn
    return pl.pallas_call(
        functools.partial(_mm_kernel, act=act),
        grid=(n // tn, m // tm),
        in_specs=[pl.BlockSpec((tm, k), lambda j, i: (i, 0)),
                  pl.BlockSpec((k, tn), lambda j, i: (0, j + joff))],
        out_specs=pl.BlockSpec((tm, tn), lambda j, i: (i, j)),
        out_shape=jax.ShapeDtypeStruct((m, n), out_dtype),
        scratch_shapes=[pltpu.VMEM((k, tn), BF16)],
        compiler_params=_params("arbitrary", "arbitrary"),
        name=name,
    )(a, w)


def _even_kernel(u_ref, v_ref, p_ref, pprev_ref, hist_ref, glng_ref, glnb_ref, ws_ref, gbt_ref,
                 pw_ref, ps_ref, *rest, tl, chunk, n_groups, pos0, emit_v):
    if emit_v:
        ab_ref, gv_ref, pwb_ref = rest
    else:
        ab_ref, pwb_ref = rest
        gv_ref = None
    b = pl.program_id(0)
    i = pl.program_id(1)
    d_a = u_ref.shape[-1]
    d_b = p_ref.shape[-1]
    ag = d_a // n_groups
    bg = d_b // len(POOL_WINDOWS)

    @pl.when((b == 0) & (i == 0))
    def _():
        pwb_ref[...] = pw_ref[...].astype(BF16)

    tri = (lax.broadcasted_iota(jnp.int32, (chunk, chunk), 0)
           >= lax.broadcasted_iota(jnp.int32, (chunk, chunk), 1))
    for g in range(n_groups):
        sl = slice(g * ag, (g + 1) * ag)
        vn = _layer_norm_rows(v_ref[:, sl], glng_ref[:, sl], glnb_ref[:, sl])
        if emit_v:
            gv_ref[:, sl] = vn
        wm = jnp.where(tri, ws_ref[g][:chunk, :chunk], 0.0).astype(BF16)
        vb = vn.astype(BF16)
        bias = gbt_ref[:chunk, g:g + 1]
        for c in range(tl // chunk):
            rows = slice(c * chunk, (c + 1) * chunk)
            mixed = jnp.dot(wm, vb[rows], preferred_element_type=F32) + bias
            ab_ref[rows, sl] = (u_ref[rows, sl] * mixed).astype(ab_ref.dtype)

    prev = jnp.where(i == 0, hist_ref[...], pprev_ref[...])
    p = p_ref[...]
    ext = jnp.concatenate([prev, p], axis=0)
    pos = pos0 + i * tl + lax.broadcasted_iota(jnp.int32, (tl, 1), 0)
    for gi, w in enumerate(POOL_WINDOWS):
        sl = slice(gi * bg, (gi + 1) * bg)
        s = ext[:, sl]
        step = 1
        while step < w:
            s = s + pltpu.roll(s, step, 0)
            step *= 2
        cnt = jnp.minimum(w, pos + 1).astype(F32)
        pooled = s[POOL_HALO:, :] / cnt - p[:, sl]
        mixed = jnp.dot(pooled.astype(BF16), pwb_ref[gi], preferred_element_type=F32) * ps_ref[:, sl]
        ab_ref[:, d_a + gi * bg:d_a + (gi + 1) * bg] = mixed.astype(ab_ref.dtype)


def _even_mix(u, v, p, hist, gln_g, gln_b, gws, gb, pool_w, pool_scale, *, pos0, emit_v):
    bn, l, d_a = u.shape
    d_b = p.shape[-1]
    n_groups = gws.shape[0]
    chunk = min(l, GMLP_CHUNK)
    tl = _tile(l, 2 * GMLP_CHUNK, chunk)
    assert l % chunk == 0 and tl % chunk == 0 and tl % POOL_HALO == 0
    halo_blocks = tl // POOL_HALO
    hist16 = jnp.concatenate(
        [jnp.zeros((bn, POOL_HALO - hist.shape[1], d_b), F32), hist], axis=1)
    out_shape = [jax.ShapeDtypeStruct((bn, l, d_a + d_b), BF16)]
    out_specs = [pl.BlockSpec((None, tl, d_a + d_b), lambda b, i: (b, i, 0))]
    if emit_v:
        out_shape.append(jax.ShapeDtypeStruct((bn, l, d_a), F32))
        out_specs.append(pl.BlockSpec((None, tl, d_a), lambda b, i: (b, i, 0)))
    row_blk = lambda b, i: (b, i, 0)
    whole2 = lambda b, i: (0, 0)
    whole3 = lambda b, i: (0, 0, 0)
    res = pl.pallas_call(
        functools.partial(_even_kernel, tl=tl, chunk=chunk, n_groups=n_groups, pos0=pos0, emit_v=emit_v),
        grid=(bn, l // tl),
        in_specs=[pl.BlockSpec((None, tl, d_a), row_blk),
                  pl.BlockSpec((None, tl, d_a), row_blk),
                  pl.BlockSpec((None, tl, d_b), row_blk),
                  pl.BlockSpec((None, POOL_HALO, d_b),
                               lambda b, i: (b, jnp.maximum(i * halo_blocks - 1, 0), 0)),
                  pl.BlockSpec((None, POOL_HALO, d_b), lambda b, i: (b, 0, 0)),
                  pl.BlockSpec((1, d_a), whole2),
                  pl.BlockSpec((1, d_a), whole2),
                  pl.BlockSpec(gws.shape, whole3),
                  pl.BlockSpec((gb.shape[1], gb.shape[0]), whole2),
                  pl.BlockSpec(pool_w.shape, whole3),
                  pl.BlockSpec((1, d_b), whole2)],
        out_specs=out_specs,
        out_shape=out_shape,
        scratch_shapes=[pltpu.VMEM(pool_w.shape, BF16)],
        compiler_params=_params("arbitrary", "arbitrary"),
        name="even_mixer",
    )(u, v, p, p, hist16, gln_g.reshape(1, d_a), gln_b.reshape(1, d_a), gws, gb.T, pool_w,
      pool_scale.reshape(1, d_b))
    return res if emit_v else (res[0], None)


def _post_norm(x, y, gate, g, b, alpha):
    return _layer_norm_rows(alpha * x + (1.0 + gate) * y, g, b)


def _pack_bf16_pairs(h):
    half = h.shape[-1] // 2
    bits = lax.bitcast_convert_type(h.astype(BF16).astype(F32), jnp.uint32)
    return (bits[:, :half] >> 16) | (bits[:, half:] & jnp.uint32(0xFFFF0000))


def _unpack_bf16_pairs(p):
    lo = lax.bitcast_convert_type(p << 16, F32)
    hi = lax.bitcast_convert_type(p & jnp.uint32(0xFFFF0000), F32)
    return jnp.concatenate([lo, hi], axis=-1).astype(BF16)


def _norm_mod_store(x, y, m, g, b, mn, xo_ref, h_ref, rows, alpha):
    xn = _post_norm(x, y, m[2:3, :], g, b, alpha)
    xo_ref[rows, :] = xn
    if h_ref is not None:
        h = xn * (1.0 + mn[1:2, :]) + mn[0:1, :]
        h_ref[rows, :] = _pack_bf16_pairs(h) if h_ref.dtype == jnp.uint32 else h.astype(h_ref.dtype)


def _tile_plan(t_p, l_p, bs, ls):
    tl = bs * ls
    assert l_p % tl == 0 and tl % 8 == 0
    n_p = t_p // tl
    per_stream = l_p // tl
    p_blk = lambda s: (jnp.minimum(s, n_p - 1), 0)
    p_mod = lambda s: (jnp.minimum(s, n_p - 1) // per_stream, 0, 0)
    return tl, n_p, p_blk, p_mod


def _lnres_kernel(xp_ref, yp_ref, mp_ref, xs_ref, ys_ref, ms_ref, g_ref, b_ref, mpn_ref, msn_ref,
                  xpo_ref, xso_ref, h_ref, *, n_p, bs, ls, alpha):
    s = pl.program_id(0)

    @pl.when(s < n_p)
    def _():
        _norm_mod_store(xp_ref[...], yp_ref[...], mp_ref[...], g_ref[...], b_ref[...], mpn_ref[...],
                        xpo_ref, h_ref, slice(None), alpha)

    @pl.when(s == n_p)
    def _():
        for q in range(bs):
            rows = slice(q * ls, (q + 1) * ls)
            _norm_mod_store(xs_ref[rows, :], ys_ref[rows, :], ms_ref[q], g_ref[...], b_ref[...], msn_ref[q],
                            xso_ref, h_ref, rows, alpha)


def _lnres(xp, yp, mp, xs, ys, ms, g, b, mpn, msn, *, l_p, alpha):
    t_p, d = xp.shape
    bs = ms.shape[0]
    ls = xs.shape[0] // bs
    tl, n_p, p_blk, p_mod = _tile_plan(t_p, l_p, bs, ls)
    whole2 = lambda s: (0, 0)
    whole3 = lambda s: (0, 0, 0)
    return pl.pallas_call(
        functools.partial(_lnres_kernel, n_p=n_p, bs=bs, ls=ls, alpha=alpha),
        grid=(n_p + 1,),
        in_specs=[pl.BlockSpec((tl, d), p_blk), pl.BlockSpec((tl, d), p_blk),
                  pl.BlockSpec((None, 3, d), p_mod),
                  pl.BlockSpec((tl, d), whole2), pl.BlockSpec((tl, d), whole2),
                  pl.BlockSpec((bs, 3, d), whole3),
                  pl.BlockSpec((1, d), whole2), pl.BlockSpec((1, d), whole2),
                  pl.BlockSpec((None, 3, d), p_mod), pl.BlockSpec((bs, 3, d), whole3)],
        out_specs=[pl.BlockSpec((tl, d), p_blk), pl.BlockSpec((tl, d), whole2),
                   pl.BlockSpec((tl, d // 2), lambda s: (s, 0))],
        out_shape=[jax.ShapeDtypeStruct((t_p, d), F32), jax.ShapeDtypeStruct((tl, d), F32),
                   jax.ShapeDtypeStruct((t_p + tl, d // 2), jnp.uint32)],
        compiler_params=_params("arbitrary"),
        name="post_norm_modulate",
    )(xp, yp, mp, xs, ys, ms, g.reshape(1, d), b.reshape(1, d), mpn, msn)


def _router_kernel(h_ref, w_ref, b_ref, idx_ref, gate_ref):
    logits = jnp.dot(_unpack_bf16_pairs(h_ref[...]), w_ref[...].astype(BF16),
                     preferred_element_type=F32) + b_ref[...]
    n_exp = logits.shape[-1]
    lane = lax.broadcasted_iota(jnp.int32, logits.shape, 1)
    vals = []
    for k in range(TOP_K):
        m = jnp.max(logits, axis=-1, keepdims=True)
        sel = jnp.min(jnp.where(logits == m, lane, n_exp), axis=-1, keepdims=True)
        idx_ref[:, k:k + 1] = sel
        vals.append(m)
        logits = jnp.where(lane == sel, -jnp.inf, logits)
    exps = [jnp.exp(v - vals[0]) for v in vals]
    denom = exps[0]
    for e in exps[1:]:
        denom = denom + e
    for k in range(TOP_K):
        gate_ref[:, k:k + 1] = exps[k] / denom


def _router(h, w_r, b_r):
    t = h.shape[0]
    d, e = w_r.shape
    tt = _tile(t, 512)
    return pl.pallas_call(
        _router_kernel,
        grid=(t // tt,),
        in_specs=[pl.BlockSpec((tt, d // 2), lambda i: (i, 0)),
                  pl.BlockSpec((d, e), lambda i: (0, 0)),
                  pl.BlockSpec((1, e), lambda i: (0, 0))],
        out_specs=[pl.BlockSpec((tt, TOP_K), lambda i: (i, 0)),
                   pl.BlockSpec((tt, TOP_K), lambda i: (i, 0))],
        out_shape=[jax.ShapeDtypeStruct((t, TOP_K), jnp.int32),
                   jax.ShapeDtypeStruct((t, TOP_K), F32)],
        compiler_params=_params("arbitrary"),
        name="router_topk",
    )(h, w_r, b_r.reshape(1, e))


GATHER_UNROLL = 8


def _row_gather(idx_smem, src_hbm, dst, sem, n, base=0, spread=False):
    assert n % GATHER_UNROLL == 0

    def body(g, carry):
        for u in range(GATHER_UNROLL):
            r = g * GATHER_UNROLL + u
            pltpu.make_async_copy(src_hbm.at[pl.ds(idx_smem[base + r], 1)], dst.at[pl.ds(r, 1)], sem).start(
                priority=u % V7X_DMA_PRIORITIES if spread else 0)
        return carry
    lax.fori_loop(0, n // GATHER_UNROLL, body, 0)


def _row_gather_wait(src_hbm, dst, sem):
    pltpu.make_async_copy(src_hbm.at[pl.ds(0, dst.shape[0])], dst, sem).wait()


def _dispatch_kernel(nu_ref, tok_hbm, h_hbm, o_ref, tok_smem, buf, sem_i, sem_g, *, tm):
    i = pl.program_id(0)

    def start_block(blk, slot):
        cp = pltpu.make_async_copy(tok_hbm.at[blk], tok_smem, sem_i)
        cp.start()
        cp.wait()
        _row_gather(tok_smem, h_hbm, buf.at[slot], sem_g.at[slot], tm, spread=True)

    @pl.when(i == 0)
    def _():
        start_block(0, 0)

    @pl.when(i + 1 < nu_ref[0])
    def _():
        start_block(i + 1, (i + 1) % 2)

    @pl.when(i < nu_ref[0])
    def _():
        slot = i % 2
        _row_gather_wait(h_hbm, buf.at[slot], sem_g.at[slot])
        o_ref[...] = _unpack_bf16_pairs(buf[slot])

    @pl.when(i >= nu_ref[0])
    def _():
        o_ref[...] = jnp.zeros_like(o_ref)


def _dispatch(h, row_tok, n_used, tm):
    t, dw = h.shape
    n_blocks = row_tok.shape[0]
    return pl.pallas_call(
        functools.partial(_dispatch_kernel, tm=tm),
        grid_spec=pltpu.PrefetchScalarGridSpec(
            num_scalar_prefetch=1,
            grid=(n_blocks,),
            in_specs=[pl.BlockSpec(memory_space=pl.ANY), pl.BlockSpec(memory_space=pl.ANY)],
            out_specs=pl.BlockSpec((tm, 2 * dw), lambda i, nu: (i, 0)),
            scratch_shapes=[pltpu.SMEM((tm,), jnp.int32), pltpu.VMEM((2, tm, dw), jnp.uint32),
                            pltpu.SemaphoreType.DMA, pltpu.SemaphoreType.DMA((2,))]),
        out_shape=jax.ShapeDtypeStruct((n_blocks * tm, 2 * dw), BF16),
        compiler_params=_params("arbitrary"),
        name="moe_dispatch",
    )(n_used, row_tok, h)


def _deinterleave_matrix():
    g = V7X_MXU_DIM
    r = lax.broadcasted_iota(jnp.int32, (g, g), 0)
    c = lax.broadcasted_iota(jnp.int32, (g, g), 1)
    src = jnp.where(c < g // 2, 2 * c, 2 * (c - g // 2) + 1)
    return (r == src).astype(BF16)


def _stream_expert_weights(be_ref, first_ref, nxt_ref, nu_ref, w_hbm, stage, sem, convert, *, layer, tn):
    j = pl.program_id(0)
    i = pl.program_id(1)

    def tile_copy(e, jj):
        c0 = pl.multiple_of(jj * tn, tn)
        return pltpu.make_async_copy(w_hbm.at[layer, e, :, pl.ds(c0, tn)], stage, sem)

    @pl.when((j == 0) & (i == 0))
    def _():
        tile_copy(be_ref[0], 0).start()

    @pl.when((i < nu_ref[0]) & (first_ref[i] == 1))
    def _():
        tile_copy(be_ref[i], j).wait()
        convert()
        nxt = nxt_ref[i]

        @pl.when(nxt >= 0)
        def _():
            tile_copy(nxt, j).start()

        @pl.when((nxt < 0) & (j + 1 < pl.num_programs(0)))
        def _():
            tile_copy(be_ref[0], j + 1).start()


def _ffn1_kernel(be_ref, first_ref, nxt_ref, nu_ref, x_ref, w_hbm, b_ref, o_ref, stage, wb_ref, sem,
                 *, layer, tn):
    i = pl.program_id(1)
    valid = i < nu_ref[0]
    grp = V7X_MXU_DIM
    half = grp // 2

    def convert():
        perm = _deinterleave_matrix()
        for g in range(tn // grp):
            cols = slice(g * grp, (g + 1) * grp)
            wg = stage[:, cols].astype(BF16)
            wb_ref[:, cols] = jnp.dot(wg, perm, preferred_element_type=F32).astype(BF16)

    _stream_expert_weights(be_ref, first_ref, nxt_ref, nu_ref, w_hbm, stage, sem, convert, layer=layer, tn=tn)

    @pl.when(valid)
    def _():
        hg = jnp.dot(x_ref[...], wb_ref[...], preferred_element_type=F32) + b_ref[...]
        for g in range(tn // grp):
            glu = jnp.minimum(hg[:, g * grp:g * grp + half], SWIGLU_LIMIT)
            lin = jnp.clip(hg[:, g * grp + half:(g + 1) * grp], -SWIGLU_LIMIT, SWIGLU_LIMIT)
            act = glu * jax.nn.sigmoid(SWIGLU_ALPHA * glu) * (lin + 1.0)
            o_ref[:, g * half:(g + 1) * half] = act.astype(o_ref.dtype)

    @pl.when(jnp.logical_not(valid))
    def _():
        o_ref[...] = jnp.zeros_like(o_ref)


def _ffn2_kernel(be_ref, first_ref, nxt_ref, nu_ref, a_ref, w_hbm, b_ref, o_ref, stage, wb_ref, sem,
                 *, layer, tn):
    i = pl.program_id(1)
    valid = i < nu_ref[0]

    def convert():
        wb_ref[...] = stage[...].astype(BF16)

    _stream_expert_weights(be_ref, first_ref, nxt_ref, nu_ref, w_hbm, stage, sem, convert, layer=layer, tn=tn)

    @pl.when(valid)
    def _():
        o_ref[...] = jnp.dot(a_ref[...], wb_ref[...], preferred_element_type=F32) + b_ref[...]

    @pl.when(jnp.logical_not(valid))
    def _():
        o_ref[...] = jnp.zeros_like(o_ref)


def _grouped_mm(kernel, x, w, b, plan, *, layer, tm, tn, n_out, out_dtype, name):
    n_rows, k = x.shape
    n_blocks = n_rows // tm
    n = w.shape[-1]
    last = lambda nu: jnp.maximum(nu[0] - 1, 0)
    x_map = lambda j, i, be, fi, nx, nu: (jnp.minimum(i, last(nu)), 0)
    b_map = lambda j, i, be, fi, nx, nu: (be[jnp.minimum(i, last(nu))], 0, j)
    o_map = lambda j, i, be, fi, nx, nu: (i, j)
    return pl.pallas_call(
        functools.partial(kernel, layer=layer, tn=tn),
        grid_spec=pltpu.PrefetchScalarGridSpec(
            num_scalar_prefetch=4,
            grid=(n // tn, n_blocks),
            in_specs=[pl.BlockSpec((tm, k), x_map),
                      pl.BlockSpec(memory_space=pl.ANY),
                      pl.BlockSpec((None, 1, tn), b_map)],
            out_specs=pl.BlockSpec((tm, n_out // (n // tn)), o_map),
            scratch_shapes=[pltpu.VMEM((k, tn), F32), pltpu.VMEM((k, tn), BF16), pltpu.SemaphoreType.DMA]),
        out_shape=jax.ShapeDtypeStruct((n_rows, n_out), out_dtype),
        compiler_params=_params("arbitrary", "arbitrary"),
        name=name,
    )(plan["block_e"], plan["is_first"], plan["next_e"], plan["n_used"], x, w, b)


def _combine_kernel(pos_hbm, y_hbm, xp_ref, xs_ref, gates_ref, mp_ref, ms_ref, g_ref, b_ref, *rest,
                    tl, n_p, bs, ls, alpha, has_next):
    if has_next:
        mpn_ref, msn_ref, xpo_ref, xso_ref, hp_ref, hs_ref, pos_smem, buf, y_buf, sem_i, sem_g = rest
    else:
        xpo_ref, xso_ref, pos_smem, buf, y_buf, sem_i, sem_g = rest
        mpn_ref = msn_ref = hp_ref = hs_ref = None
    s = pl.program_id(0)

    def start_tile(t, slot):
        cp = pltpu.make_async_copy(pos_hbm.at[t], pos_smem, sem_i)
        cp.start()
        cp.wait()
        for k in range(TOP_K):
            _row_gather(pos_smem, y_hbm, buf.at[slot, k], sem_g.at[slot], tl, base=k * tl)

    @pl.when(s == 0)
    def _():
        start_tile(0, 0)

    @pl.when(s < n_p)
    def _():
        start_tile(s + 1, (s + 1) % 2)

    slot = s % 2
    for k in range(TOP_K):
        _row_gather_wait(y_hbm, buf.at[slot, k], sem_g.at[slot])
    y = gates_ref[:, 0:1] * buf[slot, 0]
    for k in range(1, TOP_K):
        y = y + gates_ref[:, k:k + 1] * buf[slot, k]
    y_buf[...] = y

    @pl.when(s < n_p)
    def _():
        _norm_mod_store(xp_ref[...], y_buf[...], mp_ref[...], g_ref[...], b_ref[...],
                        mpn_ref[...] if has_next else None, xpo_ref, hp_ref, slice(None), alpha)

    @pl.when(s == n_p)
    def _():
        for q in range(bs):
            rows = slice(q * ls, (q + 1) * ls)
            _norm_mod_store(xs_ref[rows, :], y_buf[rows, :], ms_ref[q], g_ref[...], b_ref[...],
                            msn_ref[q] if has_next else None, xso_ref, hs_ref, rows, alpha)


def _combine(xp, xs, y_rows, pos, gates, mp, ms, g, b, mpn, msn, *, l_p, alpha, h_dtype):
    t_p, d = xp.shape
    bs = ms.shape[0]
    ls = xs.shape[0] // bs
    tl, n_p, p_blk, p_mod = _tile_plan(t_p, l_p, bs, ls)
    has_next = mpn is not None
    pos_tiles = pos.reshape(n_p + 1, tl, TOP_K).transpose(0, 2, 1).reshape(n_p + 1, TOP_K * tl)
    whole2 = lambda s: (0, 0)
    whole3 = lambda s: (0, 0, 0)
    in_specs = [pl.BlockSpec(memory_space=pl.ANY), pl.BlockSpec(memory_space=pl.ANY),
                pl.BlockSpec((tl, d), p_blk), pl.BlockSpec((tl, d), whole2),
                pl.BlockSpec((tl, TOP_K), lambda s: (s, 0)),
                pl.BlockSpec((None, 3, d), p_mod), pl.BlockSpec((bs, 3, d), whole3),
                pl.BlockSpec((1, d), whole2), pl.BlockSpec((1, d), whole2)]
    args = [pos_tiles, y_rows, xp, xs, gates, mp, ms, g.reshape(1, d), b.reshape(1, d)]
    out_specs = [pl.BlockSpec((tl, d), p_blk), pl.BlockSpec((tl, d), whole2)]
    out_shape = [jax.ShapeDtypeStruct((t_p, d), F32), jax.ShapeDtypeStruct((tl, d), F32)]
    if has_next:
        in_specs += [pl.BlockSpec((None, 3, d), p_mod), pl.BlockSpec((bs, 3, d), whole3)]
        args += [mpn, msn]
        out_specs += [pl.BlockSpec((tl, d), p_blk), pl.BlockSpec((tl, d), whole2)]
        out_shape += [jax.ShapeDtypeStruct((t_p, d), h_dtype), jax.ShapeDtypeStruct((tl, d), h_dtype)]
    res = pl.pallas_call(
        functools.partial(_combine_kernel, tl=tl, n_p=n_p, bs=bs, ls=ls, alpha=alpha, has_next=has_next),
        grid=(n_p + 1,),
        in_specs=in_specs,
        out_specs=out_specs,
        out_shape=out_shape,
        scratch_shapes=[pltpu.SMEM((TOP_K * tl,), jnp.int32), pltpu.VMEM((2, TOP_K, tl, d), F32),
                        pltpu.VMEM((tl, d), F32), pltpu.SemaphoreType.DMA, pltpu.SemaphoreType.DMA((2,))],
        compiler_params=_params("arbitrary"),
        name="moe_combine_post_norm",
    )(*args)
    return res if has_next else (res[0], res[1], None, None)


def _moe_plan(idx, n_experts, tm):
    t = idx.shape[0]
    n_assign = t * TOP_K
    flat_e = idx.reshape(-1)
    order = jnp.argsort(flat_e).astype(jnp.int32)
    counts = jnp.sum((flat_e[:, None] == jnp.arange(n_experts, dtype=jnp.int32)[None, :]).astype(jnp.int32), axis=0)
    padded = (counts + tm - 1) // tm * tm
    start = jnp.cumsum(counts) - counts
    pad_end = jnp.cumsum(padded)
    pad_start = pad_end - padded
    n_blocks = -(-(n_assign + n_experts * (tm - 1)) // tm)
    n_used = (pad_end[-1] // tm).astype(jnp.int32).reshape(1)
    blk = jnp.arange(n_blocks, dtype=jnp.int32)
    block_e = jnp.minimum(jnp.sum((blk[:, None] * tm >= pad_end[None, :]).astype(jnp.int32), axis=1),
                          n_experts - 1).astype(jnp.int32)
    is_first = jnp.concatenate([jnp.ones((1,), jnp.int32),
                                (block_e[1:] != block_e[:-1]).astype(jnp.int32)])
    run_start = jnp.where((is_first == 1) & (blk < n_used[0]), blk, n_blocks)
    later_start = jnp.concatenate([lax.cummin(run_start[::-1])[::-1][1:], jnp.full((1,), n_blocks, jnp.int32)])
    next_e = jnp.where(later_start < n_blocks, block_e[jnp.minimum(later_start, n_blocks - 1)], -1).astype(jnp.int32)
    off = jnp.arange(tm, dtype=jnp.int32)[None, :] + (blk * tm - pad_start[block_e])[:, None]
    s = jnp.clip(start[block_e][:, None] + off, 0, n_assign - 1)
    row_tok = jnp.where(off < counts[block_e][:, None], order[s] // TOP_K, 0).astype(jnp.int32)
    e_sorted = flat_e[order]
    dest = (pad_start[e_sorted] + jnp.arange(n_assign, dtype=jnp.int32) - start[e_sorted]).astype(jnp.int32)
    _, pos = lax.sort_key_val(order, dest)
    plan = dict(block_e=block_e, is_first=is_first, next_e=next_e, n_used=n_used)
    return row_tok, pos.reshape(t, TOP_K), plan


def _moe_rows(h_all, layer, w_r, b_r, w1, b1, w2, b2):
    d, n_experts = w_r.shape
    d_e2 = w1.shape[-1]
    tm = MOE_ROWS
    idx, gates = _router(h_all, w_r, b_r)
    row_tok, pos, plan = _moe_plan(idx, n_experts, tm)
    xs = _dispatch(h_all, row_tok, plan["n_used"], tm)
    grp = V7X_MXU_DIM
    b1p = b1.reshape(n_experts, d_e2 // grp, grp // 2, 2).transpose(0, 1, 3, 2).reshape(n_experts, 1, d_e2)
    tn1 = _tile(d_e2, FFN1_COLS, grp)
    act = _grouped_mm(_ffn1_kernel, xs, w1, b1p, plan, layer=layer, tm=tm, tn=tn1, n_out=d_e2 // 2,
                      out_dtype=BF16, name="moe_ffn1")
    tn2 = _tile(d, FFN2_COLS, 128)
    y_rows = _grouped_mm(_ffn2_kernel, act, w2, b2.reshape(n_experts, 1, d), plan, layer=layer, tm=tm, tn=tn2,
                         n_out=d, out_dtype=F32, name="moe_ffn2")
    return y_rows, pos, gates


def _after_matrix(n):
    return (lax.broadcasted_iota(jnp.int32, (n, n), 0) > lax.broadcasted_iota(jnp.int32, (n, n), 1)).astype(BF16)


def _sb_block(q, kblk, vblk, acc, c, u, scale, mask):
    z = lax.dot_general(q, kblk.astype(BF16), (((1,), (1,)), ((), ())),
                        preferred_element_type=F32) * scale
    sp = jnp.maximum(z, 0.0) + jnp.log(1.0 + jnp.exp(-jnp.abs(z)))
    lk = -sp if mask is None else jnp.where(mask, -sp, 0.0)
    hi = lk.astype(BF16)
    lo = (lk - hi.astype(F32)).astype(BF16)
    after = jnp.dot(hi, u, preferred_element_type=F32) + jnp.dot(lo, u, preferred_element_type=F32) + c
    w = jnp.exp(z - sp + after)
    if mask is not None:
        w = jnp.where(mask, w, 0.0)
    acc = acc + jnp.dot(w.astype(BF16), vblk.astype(BF16), preferred_element_type=F32)
    c = c + jnp.sum(lk, axis=-1, keepdims=True)
    return acc, c


def _attn_kernel(*refs, tq, tk_cache, dh, scale, has_cache):
    if has_cache:
        q_ref, k_ref, v_ref, ck_ref, cv_ref, o_ref = refs
    else:
        q_ref, k_ref, v_ref, o_ref = refs
    l = q_ref.shape[0]
    heads = [slice(h * dh, (h + 1) * dh) for h in range(q_ref.shape[1] // dh)]
    past = ck_ref.shape[0] if has_cache else 0
    u_new = _after_matrix(tq)
    u_cache = _after_matrix(tk_cache) if has_cache else None
    diag_mask = (lax.broadcasted_iota(jnp.int32, (tq, tq), 1) < lax.broadcasted_iota(jnp.int32, (tq, tq), 0))

    def q_block(qi, carry):
        r0 = pl.multiple_of(qi * tq, tq)
        qs = [q_ref[pl.ds(r0, tq), hs] for hs in heads]

        def sweep(kr, vr, tk, u, k0, accs, cs, mask):
            out = [_sb_block(q, kr[pl.ds(k0, tk), hs], vr[pl.ds(k0, tk), hs], acc, c, u, scale, mask)
                   for q, hs, acc, c in zip(qs, heads, accs, cs)]
            return [o[0] for o in out], [o[1] for o in out]

        def cond(s):
            kb, _, cs = s
            top = jnp.max(cs[0])
            for c in cs[1:]:
                top = jnp.maximum(top, jnp.max(c))
            return (kb >= 0) & (top > ATTN_DEAD_LOG)

        def body_of(kr, vr, tk, u):
            def body(s):
                kb, accs, cs = s
                accs, cs = sweep(kr, vr, tk, u, pl.multiple_of(kb * tk, tk), accs, cs, None)
                return kb - 1, accs, cs
            return body

        accs = [jnp.zeros((tq, dh), F32) for _ in heads]
        cs = [jnp.zeros((tq, 1), F32) for _ in heads]
        accs, cs = sweep(k_ref, v_ref, tq, u_new, r0, accs, cs, diag_mask)
        _, accs, cs = lax.while_loop(cond, body_of(k_ref, v_ref, tq, u_new), (qi - 1, accs, cs))
        if has_cache:
            _, accs, cs = lax.while_loop(cond, body_of(ck_ref, cv_ref, tk_cache, u_cache),
                                         (past // tk_cache - 1, accs, cs))
        for hs, acc in zip(heads, accs):
            o_ref[pl.ds(r0, tq), hs] = acc.astype(o_ref.dtype)
        return carry

    lax.fori_loop(0, l // tq, q_block, 0)


def _attention(q, k, v, cache_k, cache_v, *, bn, n_heads):
    m, d = q.shape
    l = m // bn
    dh = d // n_heads
    has_cache = cache_k is not None
    past = cache_k.shape[0] // bn if has_cache else 0
    tq = _tile(l, 256)
    tk_cache = _tile(past, 256) if has_cache else 0
    hp = ATTN_HEADS_PER_STEP if n_heads % ATTN_HEADS_PER_STEP == 0 else 1
    blk = lambda b, h: (b, h)
    in_specs = [pl.BlockSpec((l, hp * dh), blk)] * 3
    args = [q, k, v]
    if has_cache:
        in_specs += [pl.BlockSpec((past, hp * dh), blk)] * 2
        args += [cache_k, cache_v]
    return pl.pallas_call(
        functools.partial(_attn_kernel, tq=tq, tk_cache=tk_cache, dh=dh, scale=dh ** -0.5,
                          has_cache=has_cache),
        grid=(bn, n_heads // hp),
        in_specs=in_specs,
        out_specs=pl.BlockSpec((l, hp * dh), blk),
        out_shape=jax.ShapeDtypeStruct((m, d), BF16),
        compiler_params=_params("arbitrary", "arbitrary"),
        name="stick_breaking_attention",
    )(*args)


def kernel(x_prompt, x_sample, state_pool, cache_k, cache_v, c_prompt, c_sample, ada_w, ada_b, ln_g, ln_b,
           even_w_in, gmlp_ln_g, gmlp_ln_b, gmlp_ws, gmlp_b, pool_w, pool_scale, even_w_out, odd_w_qkv,
           odd_w_o, router_w, router_b, moe_w1, moe_b1, moe_w2, moe_b2):
    depth = ada_w.shape[0]
    d = x_prompt.shape[-1]
    n_heads = cache_k.shape[-2]
    d_a = gmlp_ln_g.shape[1] * gmlp_ln_g.shape[2]
    d_b = pool_scale.shape[-1]
    alpha = (2 * depth) ** 0.25
    nb_p = x_prompt.shape[0]
    trunks = [dict(x=x_prompt, b0=0, pos0=0, pool=None, ck=None, cv=None),
              dict(x=x_sample, b0=nb_p, pos0=cache_k.shape[2], pool=state_pool, ck=cache_k, cv=cache_v)]
    tp, ts = trunks
    for tr in trunks:
        tr["bn"], tr["l"] = tr["x"].shape[0], tr["x"].shape[1]
        tr["pools"], tr["gvs"], tr["ks"], tr["vs"] = [], [], [], []

    mods = _ada(jnp.concatenate([c_prompt, c_sample], axis=0), ada_w, ada_b)
    mods = mods.reshape(depth, 2, mods.shape[1], 3, d)

    def mod_of(tr, layer, sub):
        return mods[layer, sub, tr["b0"]:tr["b0"] + tr["bn"]]

    for tr in trunks:
        tr["h"] = _modulate(tr["x"], mod_of(tr, 0, 0), BF16).reshape(-1, d)
        tr["x"] = tr["x"].reshape(-1, d)

    for layer in range(depth):
        j = layer // 2
        for tr in trunks:
            bn, l = tr["bn"], tr["l"]
            h2 = tr["h"]
            if layer % 2 == 0:
                w_in = even_w_in[j]
                u = _mm(h2, w_in, 0, d_a, act="gelu", name="even_in_u").reshape(bn, l, d_a)
                v = _mm(h2, w_in, d_a, d_a, act="gelu", name="even_in_v").reshape(bn, l, d_a)
                p = _mm(h2, w_in, 2 * d_a, d_b, name="even_in_p").reshape(bn, l, d_b)
                n_hist = POOL_HALO - 1
                hist = jnp.zeros((bn, n_hist, d_b), F32) if tr["pool"] is None else tr["pool"][j]
                ab, gv = _even_mix(u, v, p, hist, gmlp_ln_g[j], gmlp_ln_b[j], gmlp_ws[j], gmlp_b[j],
                                   pool_w[j], pool_scale[j], pos0=tr["pos0"], emit_v=tr["pool"] is not None)
                tr["pools"].append(p[:, l - n_hist:] if l >= n_hist
                                   else jnp.concatenate([hist, p], axis=1)[:, -n_hist:])
                tr["gvs"].append(gv)
                y = _mm(ab.reshape(bn * l, d_a + d_b), even_w_out[j], 0, d, name="even_out")
            else:
                w_qkv = odd_w_qkv[j]
                q = _mm(h2, w_qkv, 0, d, out_dtype=BF16, name="attn_q")
                k = _mm(h2, w_qkv, d, d, name="attn_k")
                v = _mm(h2, w_qkv, 2 * d, d, name="attn_v")
                ck = None if tr["ck"] is None else tr["ck"][j].reshape(-1, d)
                cv = None if tr["cv"] is None else tr["cv"][j].reshape(-1, d)
                o = _attention(q, k, v, ck, cv, bn=bn, n_heads=n_heads)
                tr["ks"].append(k.reshape(bn, l, n_heads, d // n_heads))
                tr["vs"].append(v.reshape(bn, l, n_heads, d // n_heads))
                y = _mm(o, odd_w_o[j], 0, d, name="attn_out")
            tr["y"] = y
        tp["x"], ts["x"], h_all = _lnres(
            tp["x"], tp["y"], mod_of(tp, layer, 0), ts["x"], ts["y"], mod_of(ts, layer, 0),
            ln_g[layer, 0], ln_b[layer, 0], mod_of(tp, layer, 1), mod_of(ts, layer, 1),
            l_p=tp["l"], alpha=alpha)
        y_rows, pos, gates = _moe_rows(h_all, layer, router_w[layer], router_b[layer], moe_w1, moe_b1[layer],
                                       moe_w2, moe_b2[layer])
        last = layer + 1 == depth
        tp["x"], ts["x"], tp["h"], ts["h"] = _combine(
            tp["x"], ts["x"], y_rows, pos, gates, mod_of(tp, layer, 1), mod_of(ts, layer, 1),
            ln_g[layer, 1], ln_b[layer, 1], None if last else mod_of(tp, layer + 1, 0),
            None if last else mod_of(ts, layer + 1, 0), l_p=tp["l"], alpha=alpha, h_dtype=BF16)

    for tr in trunks:
        tr["x"] = tr["x"].reshape(tr["bn"], tr["l"], d)
    return (tp["x"], ts["x"], jnp.stack(tp["pools"]), jnp.stack(ts["pools"]), jnp.stack(ts["gvs"]),
            jnp.stack(tp["ks"]), jnp.stack(tp["vs"]), jnp.stack(ts["ks"]), jnp.stack(ts["vs"]))
```

```python
import functools

import jax
import jax.numpy as jnp
from jax import lax
from jax.experimental import pallas as pl
from jax.experimental.pallas import tpu as pltpu

F32 = jnp.float32
BF16 = jnp.bfloat16

POOL_WINDOWS = (2, 4, 8, 16)
POOL_HALO = 16
TOP_K = 4
GMLP_CHUNK = 128
SWIGLU_ALPHA = 1.702
SWIGLU_LIMIT = 7.0
LN_EPS = 1e-5

V7X_VMEM_LIMIT_BYTES = 56 * 1024 * 1024
V7X_MXU_DIM = 256

MOE_ROWS = 512
FFN1_COLS = 1024
FFN2_COLS = 2048
ATTN_HEADS_PER_STEP = 4
ATTN_DEAD_LOG = -110.0


def _tile(n, pref, mult=8):
    if n <= pref:
        return n
    t = (pref // mult) * mult
    while t >= mult:
        if n % t == 0:
            return t
        t -= mult
    return n


def _params(*sem):
    return pltpu.CompilerParams(dimension_semantics=sem, vmem_limit_bytes=V7X_VMEM_LIMIT_BYTES)


def _layer_norm_rows(t, g, b):
    mu = jnp.mean(t, axis=-1, keepdims=True)
    tc = t - mu
    var = jnp.mean(tc * tc, axis=-1, keepdims=True)
    return tc * lax.rsqrt(var + LN_EPS) * g + b


def _ada_kernel(c_ref, w_ref, b_ref, o_ref):
    c = c_ref[...]
    s = (c * jax.nn.sigmoid(c)).astype(BF16)
    o_ref[...] = jnp.dot(s, w_ref[...].astype(BF16), preferred_element_type=F32) + b_ref[...]


def _ada(c_all, ada_w, ada_b):
    nb, d = c_all.shape
    n = ada_w.shape[-1]
    w = ada_w.reshape(-1, d, n)
    b = ada_b.reshape(-1, 1, n)
    npair = w.shape[0]
    tn = _tile(n, 512, 128)
    return pl.pallas_call(
        _ada_kernel,
        grid=(npair, n // tn),
        in_specs=[pl.BlockSpec((nb, d), lambda p, j: (0, 0)),
                  pl.BlockSpec((None, d, tn), lambda p, j: (p, 0, j)),
                  pl.BlockSpec((None, 1, tn), lambda p, j: (p, 0, j))],
        out_specs=pl.BlockSpec((None, nb, tn), lambda p, j: (p, 0, j)),
        out_shape=jax.ShapeDtypeStruct((npair, nb, n), F32),
        compiler_params=_params("arbitrary", "arbitrary"),
        name="ada_modulation",
    )(c_all, w, b)


def _mod_kernel(x_ref, m_ref, h_ref):
    h_ref[...] = (x_ref[...] * (1.0 + m_ref[1:2, :]) + m_ref[0:1, :]).astype(h_ref.dtype)


def _modulate(x, mod, out_dtype):
    bn, l, d = x.shape
    tl = _tile(l, 256)
    return pl.pallas_call(
        _mod_kernel,
        grid=(bn, l // tl),
        in_specs=[pl.BlockSpec((None, tl, d), lambda b, i: (b, i, 0)),
                  pl.BlockSpec((None, 3, d), lambda b, i: (b, 0, 0))],
        out_specs=pl.BlockSpec((None, tl, d), lambda b, i: (b, i, 0)),
        out_shape=jax.ShapeDtypeStruct((bn, l, d), out_dtype),
        compiler_params=_params("arbitrary", "arbitrary"),
        name="modulate",
    )(x, mod)


def _mm_kernel(a_ref, w_ref, o_ref, wb_ref, *, act):
    @pl.when(pl.program_id(1) == 0)
    def _():
        wb_ref[...] = w_ref[...].astype(BF16)

    acc = jnp.dot(a_ref[...], wb_ref[...], preferred_element_type=F32)
    if act == "gelu":
        acc = jax.nn.gelu(acc, approximate=True)
    o_ref[...] = acc.astype(o_ref.dtype)


def _mm(a, w, col0, n, *, act=None, out_dtype=F32, name="matmul"):
    m, k = a.shape
    tm = _tile(m, 1024)
    tn = _tile(n, 512, 128)
    assert col0 % tn == 0
    joff = col0 // tn
    return pl.pallas_call(
        functools.partial(_mm_kernel, act=act),
        grid=(n // tn, m // tm),
        in_specs=[pl.BlockSpec((tm, k), lambda j, i: (i, 0)),
                  pl.BlockSpec((k, tn), lambda j, i: (0, j + joff))],
        out_specs=pl.BlockSpec((tm, tn), lambda j, i: (i, j)),
        out_shape=jax.ShapeDtypeStruct((m, n), out_dtype),
        scratch_shapes=[pltpu.VMEM((k, tn), BF16)],
        compiler_params=_params("arbitrary", "arbitrary"),
        name=name,
    )(a, w)


def _even_kernel(u_ref, v_ref, p_ref, pprev_ref, hist_ref, glng_ref, glnb_ref, ws_ref, gbt_ref,
                 pw_ref, ps_ref, *rest, tl, chunk, n_groups, pos0, emit_v):
    if emit_v:
        ab_ref, gv_ref, pwb_ref = rest
    else:
        ab_ref, pwb_ref = rest
        gv_ref = None
    b = pl.program_id(0)
    i = pl.program_id(1)
    d_a = u_ref.shape[-1]
    d_b = p_ref.shape[-1]
    ag = d_a // n_groups
    bg = d_b // len(POOL_WINDOWS)

    @pl.when((b == 0) & (i == 0))
    def _():
        pwb_ref[...] = pw_ref[...].astype(BF16)

    tri = (lax.broadcasted_iota(jnp.int32, (chunk, chunk), 0)
           >= lax.broadcasted_iota(jnp.int32, (chunk, chunk), 1))
    for g in range(n_groups):
        sl = slice(g * ag, (g + 1) * ag)
        vn = _layer_norm_rows(v_ref[:, sl], glng_ref[:, sl], glnb_ref[:, sl])
        if emit_v:
            gv_ref[:, sl] = vn
        wm = jnp.where(tri, ws_ref[g][:chunk, :chunk], 0.0).astype(BF16)
        vb = vn.astype(BF16)
        bias = gbt_ref[:chunk, g:g + 1]
        for c in range(tl // chunk):
            rows = slice(c * chunk, (c + 1) * chunk)
            mixed = jnp.dot(wm, vb[rows], preferred_element_type=F32) + bias
            ab_ref[rows, sl] = (u_ref[rows, sl] * mixed).astype(ab_ref.dtype)

    prev = jnp.where(i == 0, hist_ref[...], pprev_ref[...])
    p = p_ref[...]
    ext = jnp.concatenate([prev, p], axis=0)
    pos = pos0 + i * tl + lax.broadcasted_iota(jnp.int32, (tl, 1), 0)
    for gi, w in enumerate(POOL_WINDOWS):
        sl = slice(gi * bg, (gi + 1) * bg)
        s = ext[:, sl]
        step = 1
        while step < w:
            s = s + pltpu.roll(s, step, 0)
            step *= 2
        cnt = jnp.minimum(w, pos + 1).astype(F32)
        pooled = s[POOL_HALO:, :] / cnt - p[:, sl]
        mixed = jnp.dot(pooled.astype(BF16), pwb_ref[gi], preferred_element_type=F32) * ps_ref[:, sl]
        ab_ref[:, d_a + gi * bg:d_a + (gi + 1) * bg] = mixed.astype(ab_ref.dtype)


def _even_mix(u, v, p, hist, gln_g, gln_b, gws, gb, pool_w, pool_scale, *, pos0, emit_v):
    bn, l, d_a = u.shape
    d_b = p.shape[-1]
    n_groups = gws.shape[0]
    chunk = min(l, GMLP_CHUNK)
    tl = _tile(l, 2 * GMLP_CHUNK, chunk)
    assert l % chunk == 0 and tl % chunk == 0 and tl % POOL_HALO == 0
    halo_blocks = tl // POOL_HALO
    hist16 = jnp.concatenate(
        [jnp.zeros((bn, POOL_HALO - hist.shape[1], d_b), F32), hist], axis=1)
    out_shape = [jax.ShapeDtypeStruct((bn, l, d_a + d_b), BF16)]
    out_specs = [pl.BlockSpec((None, tl, d_a + d_b), lambda b, i: (b, i, 0))]
    if emit_v:
        out_shape.append(jax.ShapeDtypeStruct((bn, l, d_a), F32))
        out_specs.append(pl.BlockSpec((None, tl, d_a), lambda b, i: (b, i, 0)))
    row_blk = lambda b, i: (b, i, 0)
    whole2 = lambda b, i: (0, 0)
    whole3 = lambda b, i: (0, 0, 0)
    res = pl.pallas_call(
        functools.partial(_even_kernel, tl=tl, chunk=chunk, n_groups=n_groups, pos0=pos0, emit_v=emit_v),
        grid=(bn, l // tl),
        in_specs=[pl.BlockSpec((None, tl, d_a), row_blk),
                  pl.BlockSpec((None, tl, d_a), row_blk),
                  pl.BlockSpec((None, tl, d_b), row_blk),
                  pl.BlockSpec((None, POOL_HALO, d_b),
                               lambda b, i: (b, jnp.maximum(i * halo_blocks - 1, 0), 0)),
                  pl.BlockSpec((None, POOL_HALO, d_b), lambda b, i: (b, 0, 0)),
                  pl.BlockSpec((1, d_a), whole2),
                  pl.BlockSpec((1, d_a), whole2),
                  pl.BlockSpec(gws.shape, whole3),
                  pl.BlockSpec((gb.shape[1], gb.shape[0]), whole2),
                  pl.BlockSpec(pool_w.shape, whole3),
                  pl.BlockSpec((1, d_b), whole2)],
        out_specs=out_specs,
        out_shape=out_shape,
        scratch_shapes=[pltpu.VMEM(pool_w.shape, BF16)],
        compiler_params=_params("arbitrary", "arbitrary"),
        name="even_mixer",
    )(u, v, p, p, hist16, gln_g.reshape(1, d_a), gln_b.reshape(1, d_a), gws, gb.T, pool_w,
      pool_scale.reshape(1, d_b))
    return res if emit_v else (res[0], None)


def _post_norm(x, y, gate, g, b, alpha):
    return _layer_norm_rows(alpha * x + (1.0 + gate) * y, g, b)


def _pack_bf16_pairs(h):
    half = h.shape[-1] // 2
    bits = lax.bitcast_convert_type(h.astype(BF16).astype(F32), jnp.uint32)
    return (bits[:, :half] >> 16) | (bits[:, half:] & jnp.uint32(0xFFFF0000))


def _unpack_bf16_pairs(p):
    lo = lax.bitcast_convert_type(p << 16, F32)
    hi = lax.bitcast_convert_type(p & jnp.uint32(0xFFFF0000), F32)
    return jnp.concatenate([lo, hi], axis=-1).astype(BF16)


def _norm_mod_store(x, y, m, g, b, mn, xo_ref, h_ref, rows, alpha):
    xn = _post_norm(x, y, m[2:3, :], g, b, alpha)
    xo_ref[rows, :] = xn
    if h_ref is not None:
        h = xn * (1.0 + mn[1:2, :]) + mn[0:1, :]
        h_ref[rows, :] = _pack_bf16_pairs(h) if h_ref.dtype == jnp.uint32 else h.astype(h_ref.dtype)


def _tile_plan(t_p, l_p, bs, ls):
    tl = bs * ls
    assert l_p % tl == 0 and tl % 8 == 0
    n_p = t_p // tl
    per_stream = l_p // tl
    p_blk = lambda s: (jnp.minimum(s, n_p - 1), 0)
    p_mod = lambda s: (jnp.minimum(s, n_p - 1) // per_stream, 0, 0)
    return tl, n_p, p_blk, p_mod


def _lnres_kernel(xp_ref, yp_ref, mp_ref, xs_ref, ys_ref, ms_ref, g_ref, b_ref, mpn_ref, msn_ref,
                  xpo_ref, xso_ref, h_ref, *, n_p, bs, ls, alpha):
    s = pl.program_id(0)

    @pl.when(s < n_p)
    def _():
        _norm_mod_store(xp_ref[...], yp_ref[...], mp_ref[...], g_ref[...], b_ref[...], mpn_ref[...],
                        xpo_ref, h_ref, slice(None), alpha)

    @pl.when(s == n_p)
    def _():
        for q in range(bs):
            rows = slice(q * ls, (q + 1) * ls)
            _norm_mod_store(xs_ref[rows, :], ys_ref[rows, :], ms_ref[q], g_ref[...], b_ref[...], msn_ref[q],
                            xso_ref, h_ref, rows, alpha)


def _lnres(xp, yp, mp, xs, ys, ms, g, b, mpn, msn, *, l_p, alpha):
    t_p, d = xp.shape
    bs = ms.shape[0]
    ls = xs.shape[0] // bs
    tl, n_p, p_blk, p_mod = _tile_plan(t_p, l_p, bs, ls)
    whole2 = lambda s: (0, 0)
    whole3 = lambda s: (0, 0, 0)
    return pl.pallas_call(
        functools.partial(_lnres_kernel, n_p=n_p, bs=bs, ls=ls, alpha=alpha),
        grid=(n_p + 1,),
        in_specs=[pl.BlockSpec((tl, d), p_blk), pl.BlockSpec((tl, d), p_blk),
                  pl.BlockSpec((None, 3, d), p_mod),
                  pl.BlockSpec((tl, d), whole2), pl.BlockSpec((tl, d), whole2),
                  pl.BlockSpec((bs, 3, d), whole3),
                  pl.BlockSpec((1, d), whole2), pl.BlockSpec((1, d), whole2),
                  pl.BlockSpec((None, 3, d), p_mod), pl.BlockSpec((bs, 3, d), whole3)],
        out_specs=[pl.BlockSpec((tl, d), p_blk), pl.BlockSpec((tl, d), whole2),
                   pl.BlockSpec((tl, d // 2), lambda s: (s, 0))],
        out_shape=[jax.ShapeDtypeStruct((t_p, d), F32), jax.ShapeDtypeStruct((tl, d), F32),
                   jax.ShapeDtypeStruct((t_p + tl, d // 2), jnp.uint32)],
        compiler_params=_params("arbitrary"),
        name="post_norm_modulate",
    )(xp, yp, mp, xs, ys, ms, g.reshape(1, d), b.reshape(1, d), mpn, msn)


def _router_kernel(h_ref, w_ref, b_ref, idx_ref, gate_ref):
    logits = jnp.dot(_unpack_bf16_pairs(h_ref[...]), w_ref[...].astype(BF16),
                     preferred_element_type=F32) + b_ref[...]
    n_exp = logits.shape[-1]
    lane = lax.broadcasted_iota(jnp.int32, logits.shape, 1)
    vals = []
    for k in range(TOP_K):
        m = jnp.max(logits, axis=-1, keepdims=True)
        sel = jnp.min(jnp.where(logits == m, lane, n_exp), axis=-1, keepdims=True)
        idx_ref[:, k:k + 1] = sel
        vals.append(m)
        logits = jnp.where(lane == sel, -jnp.inf, logits)
    exps = [jnp.exp(v - vals[0]) for v in vals]
    denom = exps[0]
    for e in exps[1:]:
        denom = denom + e
    for k in range(TOP_K):
        gate_ref[:, k:k + 1] = exps[k] / denom


def _router(h, w_r, b_r):
    t = h.shape[0]
    d, e = w_r.shape
    tt = _tile(t, 512)
    return pl.pallas_call(
        _router_kernel,
        grid=(t // tt,),
        in_specs=[pl.BlockSpec((tt, d // 2), lambda i: (i, 0)),
                  pl.BlockSpec((d, e), lambda i: (0, 0)),
                  pl.BlockSpec((1, e), lambda i: (0, 0))],
        out_specs=[pl.BlockSpec((tt, TOP_K), lambda i: (i, 0)),
                   pl.BlockSpec((tt, TOP_K), lambda i: (i, 0))],
        out_shape=[jax.ShapeDtypeStruct((t, TOP_K), jnp.int32),
                   jax.ShapeDtypeStruct((t, TOP_K), F32)],
        compiler_params=_params("arbitrary"),
        name="router_topk",
    )(h, w_r, b_r.reshape(1, e))


GATHER_UNROLL = 8


def _row_gather(idx_smem, src_hbm, dst, sem, n, base=0):
    def body(r, carry):
        pltpu.make_async_copy(src_hbm.at[pl.ds(idx_smem[base + r], 1)], dst.at[pl.ds(r, 1)], sem).start()
        return carry
    lax.fori_loop(0, n, body, 0, unroll=GATHER_UNROLL)


def _row_gather_wait(src_hbm, dst, sem):
    pltpu.make_async_copy(src_hbm.at[pl.ds(0, dst.shape[0])], dst, sem).wait()


def _dispatch_kernel(nu_ref, tok_hbm, h_hbm, o_ref, tok_smem, buf, sem_i, sem_g, *, tm):
    i = pl.program_id(0)

    def start_block(blk, slot):
        cp = pltpu.make_async_copy(tok_hbm.at[blk], tok_smem, sem_i)
        cp.start()
        cp.wait()
        _row_gather(tok_smem, h_hbm, buf.at[slot], sem_g.at[slot], tm)

    @pl.when(i == 0)
    def _():
        start_block(0, 0)

    @pl.when(i + 1 < nu_ref[0])
    def _():
        start_block(i + 1, (i + 1) % 2)

    @pl.when(i < nu_ref[0])
    def _():
        slot = i % 2
        _row_gather_wait(h_hbm, buf.at[slot], sem_g.at[slot])
        o_ref[...] = _unpack_bf16_pairs(buf[slot])

    @pl.when(i >= nu_ref[0])
    def _():
        o_ref[...] = jnp.zeros_like(o_ref)


def _dispatch(h, row_tok, n_used, tm):
    t, dw = h.shape
    n_blocks = row_tok.shape[0]
    return pl.pallas_call(
        functools.partial(_dispatch_kernel, tm=tm),
        grid_spec=pltpu.PrefetchScalarGridSpec(
            num_scalar_prefetch=1,
            grid=(n_blocks,),
            in_specs=[pl.BlockSpec(memory_space=pl.ANY), pl.BlockSpec(memory_space=pl.ANY)],
            out_specs=pl.BlockSpec((tm, 2 * dw), lambda i, nu: (i, 0)),
            scratch_shapes=[pltpu.SMEM((tm,), jnp.int32), pltpu.VMEM((2, tm, dw), jnp.uint32),
                            pltpu.SemaphoreType.DMA, pltpu.SemaphoreType.DMA((2,))]),
        out_shape=jax.ShapeDtypeStruct((n_blocks * tm, 2 * dw), BF16),
        compiler_params=_params("arbitrary"),
        name="moe_dispatch",
    )(n_used, row_tok, h)


def _deinterleave_matrix():
    g = V7X_MXU_DIM
    r = lax.broadcasted_iota(jnp.int32, (g, g), 0)
    c = lax.broadcasted_iota(jnp.int32, (g, g), 1)
    src = jnp.where(c < g // 2, 2 * c, 2 * (c - g // 2) + 1)
    return (r == src).astype(BF16)


def _stream_expert_weights(be_ref, first_ref, nxt_ref, nu_ref, w_hbm, stage, sem, convert, *, layer, tn):
    j = pl.program_id(0)
    i = pl.program_id(1)

    def tile_copy(e, jj):
        c0 = pl.multiple_of(jj * tn, tn)
        return pltpu.make_async_copy(w_hbm.at[layer, e, :, pl.ds(c0, tn)], stage, sem)

    @pl.when((j == 0) & (i == 0))
    def _():
        tile_copy(be_ref[0], 0).start()

    @pl.when((i < nu_ref[0]) & (first_ref[i] == 1))
    def _():
        tile_copy(be_ref[i], j).wait()
        convert()
        nxt = nxt_ref[i]

        @pl.when(nxt >= 0)
        def _():
            tile_copy(nxt, j).start()

        @pl.when((nxt < 0) & (j + 1 < pl.num_programs(0)))
        def _():
            tile_copy(be_ref[0], j + 1).start()


def _ffn1_kernel(be_ref, first_ref, nxt_ref, nu_ref, x_ref, w_hbm, b_ref, o_ref, stage, wb_ref, sem,
                 *, layer, tn):
    i = pl.program_id(1)
    valid = i < nu_ref[0]
    grp = V7X_MXU_DIM
    half = grp // 2

    def convert():
        perm = _deinterleave_matrix()
        for g in range(tn // grp):
            cols = slice(g * grp, (g + 1) * grp)
            wg = stage[:, cols].astype(BF16)
            wb_ref[:, cols] = jnp.dot(wg, perm, preferred_element_type=F32).astype(BF16)

    _stream_expert_weights(be_ref, first_ref, nxt_ref, nu_ref, w_hbm, stage, sem, convert, layer=layer, tn=tn)

    @pl.when(valid)
    def _():
        hg = jnp.dot(x_ref[...], wb_ref[...], preferred_element_type=F32) + b_ref[...]
        for g in range(tn // grp):
            glu = jnp.minimum(hg[:, g * grp:g * grp + half], SWIGLU_LIMIT)
            lin = jnp.clip(hg[:, g * grp + half:(g + 1) * grp], -SWIGLU_LIMIT, SWIGLU_LIMIT)
            act = glu * jax.nn.sigmoid(SWIGLU_ALPHA * glu) * (lin + 1.0)
            o_ref[:, g * half:(g + 1) * half] = act.astype(o_ref.dtype)

    @pl.when(jnp.logical_not(valid))
    def _():
        o_ref[...] = jnp.zeros_like(o_ref)


def _ffn2_kernel(be_ref, first_ref, nxt_ref, nu_ref, a_ref, w_hbm, b_ref, o_ref, stage, wb_ref, sem,
                 *, layer, tn):
    i = pl.program_id(1)
    valid = i < nu_ref[0]

    def convert():
        wb_ref[...] = stage[...].astype(BF16)

    _stream_expert_weights(be_ref, first_ref, nxt_ref, nu_ref, w_hbm, stage, sem, convert, layer=layer, tn=tn)

    @pl.when(valid)
    def _():
        o_ref[...] = jnp.dot(a_ref[...], wb_ref[...], preferred_element_type=F32) + b_ref[...]

    @pl.when(jnp.logical_not(valid))
    def _():
        o_ref[...] = jnp.zeros_like(o_ref)


def _grouped_mm(kernel, x, w, b, plan, *, layer, tm, tn, n_out, out_dtype, name):
    n_rows, k = x.shape
    n_blocks = n_rows // tm
    n = w.shape[-1]
    last = lambda nu: jnp.maximum(nu[0] - 1, 0)
    x_map = lambda j, i, be, fi, nx, nu: (jnp.minimum(i, last(nu)), 0)
    b_map = lambda j, i, be, fi, nx, nu: (be[jnp.minimum(i, last(nu))], 0, j)
    o_map = lambda j, i, be, fi, nx, nu: (i, j)
    return pl.pallas_call(
        functools.partial(kernel, layer=layer, tn=tn),
        grid_spec=pltpu.PrefetchScalarGridSpec(
            num_scalar_prefetch=4,
            grid=(n // tn, n_blocks),
            in_specs=[pl.BlockSpec((tm, k), x_map),
                      pl.BlockSpec(memory_space=pl.ANY),
                      pl.BlockSpec((None, 1, tn), b_map)],
            out_specs=pl.BlockSpec((tm, n_out // (n // tn)), o_map),
            scratch_shapes=[pltpu.VMEM((k, tn), F32), pltpu.VMEM((k, tn), BF16), pltpu.SemaphoreType.DMA]),
        out_shape=jax.ShapeDtypeStruct((n_rows, n_out), out_dtype),
        compiler_params=_params("arbitrary", "arbitrary"),
        name=name,
    )(plan["block_e"], plan["is_first"], plan["next_e"], plan["n_used"], x, w, b)


def _combine_kernel(pos_hbm, y_hbm, xp_ref, xs_ref, gates_ref, mp_ref, ms_ref, g_ref, b_ref, *rest,
                    tl, n_p, bs, ls, alpha, has_next):
    if has_next:
        mpn_ref, msn_ref, xpo_ref, xso_ref, hp_ref, hs_ref, pos_smem, buf, y_buf, sem_i, sem_g = rest
    else:
        xpo_ref, xso_ref, pos_smem, buf, y_buf, sem_i, sem_g = rest
        mpn_ref = msn_ref = hp_ref = hs_ref = None
    s = pl.program_id(0)

    def start_tile(t, slot):
        cp = pltpu.make_async_copy(pos_hbm.at[t], pos_smem, sem_i)
        cp.start()
        cp.wait()
        for k in range(TOP_K):
            _row_gather(pos_smem, y_hbm, buf.at[slot, k], sem_g.at[slot], tl, base=k * tl)

    @pl.when(s == 0)
    def _():
        start_tile(0, 0)

    @pl.when(s < n_p)
    def _():
        start_tile(s + 1, (s + 1) % 2)

    slot = s % 2
    for k in range(TOP_K):
        _row_gather_wait(y_hbm, buf.at[slot, k], sem_g.at[slot])
    y = gates_ref[:, 0:1] * buf[slot, 0]
    for k in range(1, TOP_K):
        y = y + gates_ref[:, k:k + 1] * buf[slot, k]
    y_buf[...] = y

    @pl.when(s < n_p)
    def _():
        _norm_mod_store(xp_ref[...], y_buf[...], mp_ref[...], g_ref[...], b_ref[...],
                        mpn_ref[...] if has_next else None, xpo_ref, hp_ref, slice(None), alpha)

    @pl.when(s == n_p)
    def _():
        for q in range(bs):
            rows = slice(q * ls, (q + 1) * ls)
            _norm_mod_store(xs_ref[rows, :], y_buf[rows, :], ms_ref[q], g_ref[...], b_ref[...],
                            msn_ref[q] if has_next else None, xso_ref, hs_ref, rows, alpha)


def _combine(xp, xs, y_rows, pos, gates, mp, ms, g, b, mpn, msn, *, l_p, alpha, h_dtype):
    t_p, d = xp.shape
    bs = ms.shape[0]
    ls = xs.shape[0] // bs
    tl, n_p, p_blk, p_mod = _tile_plan(t_p, l_p, bs, ls)
    has_next = mpn is not None
    pos_tiles = pos.reshape(n_p + 1, tl, TOP_K).transpose(0, 2, 1).reshape(n_p + 1, TOP_K * tl)
    whole2 = lambda s: (0, 0)
    whole3 = lambda s: (0, 0, 0)
    in_specs = [pl.BlockSpec(memory_space=pl.ANY), pl.BlockSpec(memory_space=pl.ANY),
                pl.BlockSpec((tl, d), p_blk), pl.BlockSpec((tl, d), whole2),
                pl.BlockSpec((tl, TOP_K), lambda s: (s, 0)),
                pl.BlockSpec((None, 3, d), p_mod), pl.BlockSpec((bs, 3, d), whole3),
                pl.BlockSpec((1, d), whole2), pl.BlockSpec((1, d), whole2)]
    args = [pos_tiles, y_rows, xp, xs, gates, mp, ms, g.reshape(1, d), b.reshape(1, d)]
    out_specs = [pl.BlockSpec((tl, d), p_blk), pl.BlockSpec((tl, d), whole2)]
    out_shape = [jax.ShapeDtypeStruct((t_p, d), F32), jax.ShapeDtypeStruct((tl, d), F32)]
    if has_next:
        in_specs += [pl.BlockSpec((None, 3, d), p_mod), pl.BlockSpec((bs, 3, d), whole3)]
        args += [mpn, msn]
        out_specs += [pl.BlockSpec((tl, d), p_blk), pl.BlockSpec((tl, d), whole2)]
        out_shape += [jax.ShapeDtypeStruct((t_p, d), h_dtype), jax.ShapeDtypeStruct((tl, d), h_dtype)]
    res = pl.pallas_call(
        functools.partial(_combine_kernel, tl=tl, n_p=n_p, bs=bs, ls=ls, alpha=alpha, has_next=has_next),
        grid=(n_p + 1,),
        in_specs=in_specs,
        out_specs=out_specs,
        out_shape=out_shape,
        scratch_shapes=[pltpu.SMEM((TOP_K * tl,), jnp.int32), pltpu.VMEM((2, TOP_K, tl, d), F32),
                        pltpu.VMEM((tl, d), F32), pltpu.SemaphoreType.DMA, pltpu.SemaphoreType.DMA((2,))],
        compiler_params=_params("arbitrary"),
        name="moe_combine_post_norm",
    )(*args)
    return res if has_next else (res[0], res[1], None, None)


def _moe_plan(idx, n_experts, tm):
    t = idx.shape[0]
    n_assign = t * TOP_K
    flat_e = idx.reshape(-1)
    order = jnp.argsort(flat_e).astype(jnp.int32)
    counts = jnp.sum((flat_e[:, None] == jnp.arange(n_experts, dtype=jnp.int32)[None, :]).astype(jnp.int32), axis=0)
    padded = (counts + tm - 1) // tm * tm
    start = jnp.cumsum(counts) - counts
    pad_end = jnp.cumsum(padded)
    pad_start = pad_end - padded
    n_blocks = -(-(n_assign + n_experts * (tm - 1)) // tm)
    n_used = (pad_end[-1] // tm).astype(jnp.int32).reshape(1)
    blk = jnp.arange(n_blocks, dtype=jnp.int32)
    block_e = jnp.minimum(jnp.sum((blk[:, None] * tm >= pad_end[None, :]).astype(jnp.int32), axis=1),
                          n_experts - 1).astype(jnp.int32)
    is_first = jnp.concatenate([jnp.ones((1,), jnp.int32),
                                (block_e[1:] != block_e[:-1]).astype(jnp.int32)])
    run_start = jnp.where((is_first == 1) & (blk < n_used[0]), blk, n_blocks)
    later_start = jnp.concatenate([lax.cummin(run_start[::-1])[::-1][1:], jnp.full((1,), n_blocks, jnp.int32)])
    next_e = jnp.where(later_start < n_blocks, block_e[jnp.minimum(later_start, n_blocks - 1)], -1).astype(jnp.int32)
    off = jnp.arange(tm, dtype=jnp.int32)[None, :] + (blk * tm - pad_start[block_e])[:, None]
    s = jnp.clip(start[block_e][:, None] + off, 0, n_assign - 1)
    row_tok = jnp.where(off < counts[block_e][:, None], order[s] // TOP_K, 0).astype(jnp.int32)
    e_sorted = flat_e[order]
    dest = (pad_start[e_sorted] + jnp.arange(n_assign, dtype=jnp.int32) - start[e_sorted]).astype(jnp.int32)
    _, pos = lax.sort_key_val(order, dest)
    plan = dict(block_e=block_e, is_first=is_first, next_e=next_e, n_used=n_used)
    return row_tok, pos.reshape(t, TOP_K), plan


def _moe_rows(h_all, layer, w_r, b_r, w1, b1, w2, b2):
    d, n_experts = w_r.shape
    d_e2 = w1.shape[-1]
    tm = MOE_ROWS
    idx, gates = _router(h_all, w_r, b_r)
    row_tok, pos, plan = _moe_plan(idx, n_experts, tm)
    xs = _dispatch(h_all, row_tok, plan["n_used"], tm)
    grp = V7X_MXU_DIM
    b1p = b1.reshape(n_experts, d_e2 // grp, grp // 2, 2).transpose(0, 1, 3, 2).reshape(n_experts, 1, d_e2)
    tn1 = _tile(d_e2, FFN1_COLS, grp)
    act = _grouped_mm(_ffn1_kernel, xs, w1, b1p, plan, layer=layer, tm=tm, tn=tn1, n_out=d_e2 // 2,
                      out_dtype=BF16, name="moe_ffn1")
    tn2 = _tile(d, FFN2_COLS, 128)
    y_rows = _grouped_mm(_ffn2_kernel, act, w2, b2.reshape(n_experts, 1, d), plan, layer=layer, tm=tm, tn=tn2,
                         n_out=d, out_dtype=F32, name="moe_ffn2")
    return y_rows, pos, gates


def _after_matrix(n):
    return (lax.broadcasted_iota(jnp.int32, (n, n), 0) > lax.broadcasted_iota(jnp.int32, (n, n), 1)).astype(BF16)


def _sb_block(q, kblk, vblk, acc, c, u, scale, mask):
    z = lax.dot_general(q, kblk.astype(BF16), (((1,), (1,)), ((), ())),
                        preferred_element_type=F32) * scale
    sp = jnp.maximum(z, 0.0) + jnp.log(1.0 + jnp.exp(-jnp.abs(z)))
    lk = -sp if mask is None else jnp.where(mask, -sp, 0.0)
    hi = lk.astype(BF16)
    lo = (lk - hi.astype(F32)).astype(BF16)
    after = jnp.dot(hi, u, preferred_element_type=F32) + jnp.dot(lo, u, preferred_element_type=F32) + c
    w = jnp.exp(z - sp + after)
    if mask is not None:
        w = jnp.where(mask, w, 0.0)
    acc = acc + jnp.dot(w.astype(BF16), vblk.astype(BF16), preferred_element_type=F32)
    c = c + jnp.sum(lk, axis=-1, keepdims=True)
    return acc, c


def _attn_kernel(*refs, tq, tk_cache, dh, scale, has_cache):
    if has_cache:
        q_ref, k_ref, v_ref, ck_ref, cv_ref, o_ref = refs
    else:
        q_ref, k_ref, v_ref, o_ref = refs
    l = q_ref.shape[0]
    heads = [slice(h * dh, (h + 1) * dh) for h in range(q_ref.shape[1] // dh)]
    past = ck_ref.shape[0] if has_cache else 0
    u_new = _after_matrix(tq)
    u_cache = _after_matrix(tk_cache) if has_cache else None
    diag_mask = (lax.broadcasted_iota(jnp.int32, (tq, tq), 1) < lax.broadcasted_iota(jnp.int32, (tq, tq), 0))

    def q_block(qi, carry):
        r0 = pl.multiple_of(qi * tq, tq)
        qs = [q_ref[pl.ds(r0, tq), hs] for hs in heads]

        def sweep(kr, vr, tk, u, k0, accs, cs, mask):
            out = [_sb_block(q, kr[pl.ds(k0, tk), hs], vr[pl.ds(k0, tk), hs], acc, c, u, scale, mask)
                   for q, hs, acc, c in zip(qs, heads, accs, cs)]
            return [o[0] for o in out], [o[1] for o in out]

        def cond(s):
            kb, _, cs = s
            top = jnp.max(cs[0])
            for c in cs[1:]:
                top = jnp.maximum(top, jnp.max(c))
            return (kb >= 0) & (top > ATTN_DEAD_LOG)

        def body_of(kr, vr, tk, u):
            def body(s):
                kb, accs, cs = s
                accs, cs = sweep(kr, vr, tk, u, pl.multiple_of(kb * tk, tk), accs, cs, None)
                return kb - 1, accs, cs
            return body

        accs = [jnp.zeros((tq, dh), F32) for _ in heads]
        cs = [jnp.zeros((tq, 1), F32) for _ in heads]
        accs, cs = sweep(k_ref, v_ref, tq, u_new, r0, accs, cs, diag_mask)
        _, accs, cs = lax.while_loop(cond, body_of(k_ref, v_ref, tq, u_new), (qi - 1, accs, cs))
        if has_cache:
            _, accs, cs = lax.while_loop(cond, body_of(ck_ref, cv_ref, tk_cache, u_cache),
                                         (past // tk_cache - 1, accs, cs))
        for hs, acc in zip(heads, accs):
            o_ref[pl.ds(r0, tq), hs] = acc.astype(o_ref.dtype)
        return carry

    lax.fori_loop(0, l // tq, q_block, 0)


def _attention(q, k, v, cache_k, cache_v, *, bn, n_heads):
    m, d = q.shape
    l = m // bn
    dh = d // n_heads
    has_cache = cache_k is not None
    past = cache_k.shape[0] // bn if has_cache else 0
    tq = _tile(l, 256)
    tk_cache = _tile(past, 256) if has_cache else 0
    hp = ATTN_HEADS_PER_STEP if n_heads % ATTN_HEADS_PER_STEP == 0 else 1
    blk = lambda b, h: (b, h)
    in_specs = [pl.BlockSpec((l, hp * dh), blk)] * 3
    args = [q, k, v]
    if has_cache:
        in_specs += [pl.BlockSpec((past, hp * dh), blk)] * 2
        args += [cache_k, cache_v]
    return pl.pallas_call(
        functools.partial(_attn_kernel, tq=tq, tk_cache=tk_cache, dh=dh, scale=dh ** -0.5,
                          has_cache=has_cache),
        grid=(bn, n_heads // hp),
        in_specs=in_specs,
        out_specs=pl.BlockSpec((l, hp * dh), blk),
        out_shape=jax.ShapeDtypeStruct((m, d), BF16),
        compiler_params=_params("arbitrary", "arbitrary"),
        name="stick_breaking_attention",
    )(*args)


def kernel(x_prompt, x_sample, state_pool, cache_k, cache_v, c_prompt, c_sample, ada_w, ada_b, ln_g, ln_b,
           even_w_in, gmlp_ln_g, gmlp_ln_b, gmlp_ws, gmlp_b, pool_w, pool_scale, even_w_out, odd_w_qkv,
           odd_w_o, router_w, router_b, moe_w1, moe_b1, moe_w2, moe_b2):
    depth = ada_w.shape[0]
    d = x_prompt.shape[-1]
    n_heads = cache_k.shape[-2]
    d_a = gmlp_ln_g.shape[1] * gmlp_ln_g.shape[2]
    d_b = pool_scale.shape[-1]
    alpha = (2 * depth) ** 0.25
    nb_p = x_prompt.shape[0]
    trunks = [dict(x=x_prompt, b0=0, pos0=0, pool=None, ck=None, cv=None),
              dict(x=x_sample, b0=nb_p, pos0=cache_k.shape[2], pool=state_pool, ck=cache_k, cv=cache_v)]
    tp, ts = trunks
    for tr in trunks:
        tr["bn"], tr["l"] = tr["x"].shape[0], tr["x"].shape[1]
        tr["pools"], tr["gvs"], tr["ks"], tr["vs"] = [], [], [], []

    mods = _ada(jnp.concatenate([c_prompt, c_sample], axis=0), ada_w, ada_b)
    mods = mods.reshape(depth, 2, mods.shape[1], 3, d)

    def mod_of(tr, layer, sub):
        return mods[layer, sub, tr["b0"]:tr["b0"] + tr["bn"]]

    for tr in trunks:
        tr["h"] = _modulate(tr["x"], mod_of(tr, 0, 0), BF16).reshape(-1, d)
        tr["x"] = tr["x"].reshape(-1, d)

    for layer in range(depth):
        j = layer // 2
        for tr in trunks:
            bn, l = tr["bn"], tr["l"]
            h2 = tr["h"]
            if layer % 2 == 0:
                w_in = even_w_in[j]
                u = _mm(h2, w_in, 0, d_a, act="gelu", name="even_in_u").reshape(bn, l, d_a)
                v = _mm(h2, w_in, d_a, d_a, act="gelu", name="even_in_v").reshape(bn, l, d_a)
                p = _mm(h2, w_in, 2 * d_a, d_b, name="even_in_p").reshape(bn, l, d_b)
                n_hist = POOL_HALO - 1
                hist = jnp.zeros((bn, n_hist, d_b), F32) if tr["pool"] is None else tr["pool"][j]
                ab, gv = _even_mix(u, v, p, hist, gmlp_ln_g[j], gmlp_ln_b[j], gmlp_ws[j], gmlp_b[j],
                                   pool_w[j], pool_scale[j], pos0=tr["pos0"], emit_v=tr["pool"] is not None)
                tr["pools"].append(p[:, l - n_hist:] if l >= n_hist
                                   else jnp.concatenate([hist, p], axis=1)[:, -n_hist:])
                tr["gvs"].append(gv)
                y = _mm(ab.reshape(bn * l, d_a + d_b), even_w_out[j], 0, d, name="even_out")
            else:
                w_qkv = odd_w_qkv[j]
                q = _mm(h2, w_qkv, 0, d, out_dtype=BF16, name="attn_q")
                k = _mm(h2, w_qkv, d, d, name="attn_k")
                v = _mm(h2, w_qkv, 2 * d, d, name="attn_v")
                ck = None if tr["ck"] is None else tr["ck"][j].reshape(-1, d)
                cv = None if tr["cv"] is None else tr["cv"][j].reshape(-1, d)
                o = _attention(q, k, v, ck, cv, bn=bn, n_heads=n_heads)
                tr["ks"].append(k.reshape(bn, l, n_heads, d // n_heads))
                tr["vs"].append(v.reshape(bn, l, n_heads, d // n_heads))
                y = _mm(o, odd_w_o[j], 0, d, name="attn_out")
            tr["y"] = y
        tp["x"], ts["x"], h_all = _lnres(
            tp["x"], tp["y"], mod_of(tp, layer, 0), ts["x"], ts["y"], mod_of(ts, layer, 0),
            ln_g[layer, 0], ln_b[layer, 0], mod_of(tp, layer, 1), mod_of(ts, layer, 1),
            l_p=tp["l"], alpha=alpha)
        y_rows, pos, gates = _moe_rows(h_all, layer, router_w[layer], router_b[layer], moe_w1, moe_b1[layer],
                                       moe_w2, moe_b2[layer])
        last = layer + 1 == depth
        tp["x"], ts["x"], tp["h"], ts["h"] = _combine(
            tp["x"], ts["x"], y_rows, pos, gates, mod_of(tp, layer, 1), mod_of(ts, layer, 1),
            ln_g[layer, 1], ln_b[layer, 1], None if last else mod_of(tp, layer + 1, 0),
            None if last else mod_of(ts, layer + 1, 0), l_p=tp["l"], alpha=alpha, h_dtype=BF16)

    for tr in trunks:
        tr["x"] = tr["x"].reshape(tr["bn"], tr["l"], d)
    return (tp["x"], ts["x"], jnp.stack(tp["pools"]), jnp.stack(ts["pools"]), jnp.stack(ts["gvs"]),
            jnp.stack(tp["ks"]), jnp.stack(tp["vs"]), jnp.stack(ts["ks"]), jnp.stack(ts["vs"]))
```
